```python
import math
import jax, jax.numpy as jnp
from jax import lax
import numpy as np

D_MODEL = 1024
BATCH = 8
SEQ = 2048
DEPTH = 4

N_MIXERS = 2
HEAD_DIM = 64
N_HEADS_A = 16
N_KV_HEADS_A = 4
GROUP_A = N_HEADS_A // N_KV_HEADS_A
WINDOW = 128
BLOCK_A = WINDOW
N_HEADS_B = 16
BLOCK_B = 128
D_FF = 4 * D_MODEL
N_BUCKETS = 32
MAX_DISTANCE = 128
EPS = 1e-6
NEG = -1e30
N_LAYERS_A = (DEPTH + 1) // 2
N_LAYERS_B = DEPTH // 2
QKV_A = (N_HEADS_A + 2 * N_KV_HEADS_A) * HEAD_DIM
HD_B = N_HEADS_B * HEAD_DIM
QKVF_B = 3 * HD_B + N_HEADS_B
FORGET_BIAS_MEAN = 3.0

kernel_name = "hybrid_swa_sink_fox_sqrelu"


def rmsnorm(x, g):
    xf = x.astype(jnp.float32)
    y = xf * lax.rsqrt(jnp.mean(xf * xf, axis=-1, keepdims=True) + EPS)
    return (y * g.astype(jnp.float32)).astype(x.dtype)


def t5_bucket(dist):
    max_exact = N_BUCKETS // 2
    d = jnp.maximum(dist, 0)
    dl = jnp.maximum(d, 1).astype(jnp.float32)
    large = max_exact + (jnp.log(dl / max_exact) / math.log(MAX_DISTANCE / max_exact)
                         * (N_BUCKETS - max_exact)).astype(jnp.int32)
    large = jnp.minimum(large, N_BUCKETS - 1)
    return jnp.where(d < max_exact, d, large)


def swa_sink_attention(h, w_qkv, b_qkv, w_o, b_o, sinks, rel_bias):
    B, S, _ = h.shape
    nb = S // BLOCK_A
    qkv = h @ w_qkv + b_qkv
    q, k, v = jnp.split(qkv, [N_HEADS_A * HEAD_DIM, (N_HEADS_A + N_KV_HEADS_A) * HEAD_DIM], axis=-1)
    q = q.reshape(B, nb, BLOCK_A, N_KV_HEADS_A, GROUP_A, HEAD_DIM)
    k = k.reshape(B, nb, BLOCK_A, N_KV_HEADS_A, HEAD_DIM)
    v = v.reshape(B, nb, BLOCK_A, N_KV_HEADS_A, HEAD_DIM)
    pad = jnp.zeros_like(k[:, :1])
    k2 = jnp.concatenate([jnp.concatenate([pad, k[:, :-1]], axis=1), k], axis=2)
    v2 = jnp.concatenate([jnp.concatenate([pad, v[:, :-1]], axis=1), v], axis=2)
    scale = 1.0 / math.sqrt(HEAD_DIM)
    scores = jnp.einsum('bnqhgd,bnkhd->bnhgqk', q, k2).astype(jnp.float32) * scale
    qi = jnp.arange(BLOCK_A, dtype=jnp.int32)[:, None]
    kj = jnp.arange(2 * BLOCK_A, dtype=jnp.int32)[None, :]
    dist = qi + BLOCK_A - kj
    bias = rel_bias[t5_bucket(dist)].astype(jnp.float32)
    bias = bias.transpose(2, 0, 1).reshape(N_KV_HEADS_A, GROUP_A, BLOCK_A, 2 * BLOCK_A)
    in_window = (dist >= 0) & (dist < WINDOW)
    blk = jnp.arange(nb, dtype=jnp.int32)[:, None, None]
    valid = in_window[None] & ((blk > 0) | (kj >= BLOCK_A)[None])
    scores = jnp.where(valid[None, :, None, None], scores + bias, NEG)
    sink = jnp.broadcast_to(sinks.astype(jnp.float32).reshape(N_KV_HEADS_A, GROUP_A, 1, 1),
                            scores.shape[:-1] + (1,))
    probs = jax.nn.softmax(jnp.concatenate([scores, sink], axis=-1), axis=-1)[..., :-1]
    out = jnp.einsum('bnhgqk,bnkhd->bnqhgd', probs.astype(v2.dtype), v2)
    out = out.reshape(B, S, N_HEADS_A * HEAD_DIM)
    return out @ w_o + b_o


def forgetting_attention(h, w_qkvf, b_f, w_o):
    B, S, _ = h.shape
    proj = h @ w_qkvf
    q, k, v, fz = jnp.split(proj, [HD_B, 2 * HD_B, 3 * HD_B], axis=-1)
    q = q.reshape(B, S, N_HEADS_B, HEAD_DIM)
    k = k.reshape(B, S, N_HEADS_B, HEAD_DIM)
    v = v.reshape(B, S, N_HEADS_B, HEAD_DIM)
    log_f = jax.nn.log_sigmoid((fz + b_f).astype(jnp.float32))
    c = jnp.cumsum(log_f, axis=1).transpose(0, 2, 1)
    scale = 1.0 / math.sqrt(HEAD_DIM)
    outs = []
    for n in range(S // BLOCK_B):
        q0, q1 = n * BLOCK_B, (n + 1) * BLOCK_B
        s = jnp.einsum('bqhd,bkhd->bhqk', q[:, q0:q1], k[:, :q1]).astype(jnp.float32) * scale
        decay = c[:, :, q0:q1, None] - c[:, :, None, :q1]
        causal = (q0 + jnp.arange(BLOCK_B)[:, None]) >= jnp.arange(q1)[None, :]
        p = jax.nn.softmax(jnp.where(causal, s + decay, NEG), axis=-1)
        outs.append(jnp.einsum('bhqk,bkhd->bqhd', p.astype(v.dtype), v[:, :q1]))
    out = jnp.concatenate(outs, axis=1).reshape(B, S, HD_B)
    return out @ w_o


def setup_inputs(seed: int = 0) -> dict:
    key = jax.random.key(seed)
    ks = jax.random.split(key, 16)
    nrm = jax.random.normal
    D = D_MODEL
    x = nrm(ks[0], (BATCH, SEQ, D), jnp.float32)
    rel_bias = 0.5 * nrm(ks[1], (N_BUCKETS, N_HEADS_A), jnp.float32)
    norm_mix = 1.0 + 0.05 * nrm(ks[2], (DEPTH, D), jnp.float32)
    norm_mlp = 1.0 + 0.05 * nrm(ks[3], (DEPTH, D), jnp.float32)
    w_qkv_a = nrm(ks[4], (N_LAYERS_A, D, QKV_A), jnp.float32) * D ** -0.5
    b_qkv_a = 0.02 * nrm(ks[5], (N_LAYERS_A, QKV_A), jnp.float32)
    sinks_a = 0.5 * nrm(ks[6], (N_LAYERS_A, N_HEADS_A), jnp.float32)
    w_o_a = nrm(ks[7], (N_LAYERS_A, N_HEADS_A * HEAD_DIM, D), jnp.float32) * (N_HEADS_A * HEAD_DIM) ** -0.5
    b_o_a = 0.02 * nrm(ks[8], (N_LAYERS_A, D), jnp.float32)
    col_scale = jnp.concatenate([jnp.ones((3 * HD_B,), jnp.float32),
                                 0.1 * jnp.ones((N_HEADS_B,), jnp.float32)])
    w_qkvf_b = nrm(ks[9], (N_LAYERS_B, D, QKVF_B), jnp.float32) * D ** -0.5 * col_scale
    b_f_b = FORGET_BIAS_MEAN + 0.5 * nrm(ks[10], (N_LAYERS_B, N_HEADS_B), jnp.float32)
    w_o_b = nrm(ks[11], (N_LAYERS_B, HD_B, D), jnp.float32) * HD_B ** -0.5
    w_up = nrm(ks[12], (DEPTH, D, D_FF), jnp.float32) * D ** -0.5
    w_down = nrm(ks[13], (DEPTH, D_FF, D), jnp.float32) * D_FF ** -0.5
    norm_final = 1.0 + 0.05 * nrm(ks[14], (D,), jnp.float32)
    return {"x": x, "rel_bias": rel_bias, "norm_mix": norm_mix, "norm_mlp": norm_mlp,
            "w_qkv_a": w_qkv_a, "b_qkv_a": b_qkv_a, "sinks_a": sinks_a, "w_o_a": w_o_a,
            "b_o_a": b_o_a, "w_qkvf_b": w_qkvf_b, "b_f_b": b_f_b, "w_o_b": w_o_b,
            "w_up": w_up, "w_down": w_down, "norm_final": norm_final}


def reference(x, rel_bias, norm_mix, norm_mlp, w_qkv_a, b_qkv_a, sinks_a, w_o_a, b_o_a,
              w_qkvf_b, b_f_b, w_o_b, w_up, w_down, norm_final):
    for i in range(DEPTH):
        h = rmsnorm(x, norm_mix[i])
        j = i // N_MIXERS
        if i % N_MIXERS == 0:
            x = x + swa_sink_attention(h, w_qkv_a[j], b_qkv_a[j], w_o_a[j], b_o_a[j],
                                       sinks_a[j], rel_bias)
        else:
            x = x + forgetting_attention(h, w_qkvf_b[j], b_f_b[j], w_o_b[j])
        h = rmsnorm(x, norm_mlp[i])
        x = x + jnp.square(jax.nn.relu(h @ w_up[i])) @ w_down[i]
    return rmsnorm(x, norm_final)
```

```python
import functools
import math

import numpy as np
import jax
import jax.numpy as jnp
from jax import lax
from jax.experimental import pallas as pl
from jax.experimental.pallas import tpu as pltpu

D_MODEL = 1024
HEAD_DIM = 64
N_HEADS = 16
N_KV_HEADS_A = 4
GROUP_A = N_HEADS // N_KV_HEADS_A
WINDOW = 128
D_FF = 4 * D_MODEL
N_BUCKETS = 32
MAX_DISTANCE = 128
EPS = 1e-6
NEG = -1e30
SCALE = 1.0 / math.sqrt(HEAD_DIM)

LANES = 128
N_PAIRS = N_HEADS // 2
VMEM_LIMIT_BYTES = 56 * 1024 * 1024

TM_PROJ = 512
TM_POST = 512
FF_CHUNK = 1024
TQ_FOX = 256
TK_FOX = 256

BF16 = jnp.bfloat16
F32 = jnp.float32


def _resident(shape):
    zeros = (0,) * len(shape)
    return pl.BlockSpec(shape, lambda *_: zeros, pipeline_mode=pl.Buffered(1))


def _rmsnorm(x, g):
    return x * lax.rsqrt(jnp.mean(x * x, axis=-1, keepdims=True) + EPS) * g


def _proj_chunks(h, w_ref, b_ref, o_ref, n_out, chunk):
    for c in range(n_out // chunk):
        sl = slice(c * chunk, (c + 1) * chunk)
        acc = jnp.dot(h, w_ref[:, sl], preferred_element_type=F32)
        if b_ref is not None:
            acc = acc + b_ref[:, sl]
        o_ref[:, sl] = acc.astype(o_ref.dtype)


def _swa_proj_kernel(x_ref, g_ref, w_ref, b_ref, o_ref):
    h = _rmsnorm(x_ref[...], g_ref[...]).astype(BF16)
    _proj_chunks(h, w_ref, b_ref, o_ref, o_ref.shape[1], 512)


def _swa_proj(x, g, w, b):
    m, d = x.shape
    n = w.shape[1]
    return pl.pallas_call(
        _swa_proj_kernel,
        grid=(m // TM_PROJ,),
        in_specs=[
            pl.BlockSpec((TM_PROJ, d), lambda i: (i, 0)),
            _resident((1, d)),
            _resident((d, n)),
            _resident((1, n)),
        ],
        out_specs=pl.BlockSpec((TM_PROJ, n), lambda i: (i, 0)),
        out_shape=jax.ShapeDtypeStruct((m, n), BF16),
        compiler_params=pltpu.CompilerParams(
            dimension_semantics=("arbitrary",), vmem_limit_bytes=VMEM_LIMIT_BYTES),
        name="swa_proj",
    )(x, g.reshape(1, d), w, b.reshape(1, n))


def _fox_proj_kernel(tiles_per_seq, x_ref, g_ref, w_ref, wf_ref, bf_ref,
                     o_ref, c_ref, ct_ref, carry_ref):
    i = pl.program_id(0)

    @pl.when(i % tiles_per_seq == 0)
    def _():
        carry_ref[...] = jnp.zeros_like(carry_ref)

    h = _rmsnorm(x_ref[...], g_ref[...]).astype(BF16)
    _proj_chunks(h, w_ref, None, o_ref, o_ref.shape[1], 512)

    fz = jnp.dot(h, wf_ref[...], preferred_element_type=F32) + bf_ref[...]
    cs = jnp.minimum(fz, 0.0) - jnp.log1p(jnp.exp(-jnp.abs(fz)))
    tm = cs.shape[0]
    row = lax.broadcasted_iota(jnp.int32, cs.shape, 0)
    shift = 1
    while shift < tm:
        cs = cs + jnp.where(row >= shift, pltpu.roll(cs, shift, axis=0), 0.0)
        shift *= 2
    cs = cs + carry_ref[...]
    carry_ref[...] = cs[tm - 1:tm, :]
    c_ref[...] = cs
    for j in range(tm // TK_FOX):
        ct_ref[j] = cs[j * TK_FOX:(j + 1) * TK_FOX, :].T[:N_HEADS, :]


def _fox_proj(x, g, w, wf, bf, seq):
    m, d = x.shape
    n = w.shape[1]
    kb_per_tile = TM_PROJ // TK_FOX
    return pl.pallas_call(
        functools.partial(_fox_proj_kernel, seq // TM_PROJ),
        grid=(m // TM_PROJ,),
        in_specs=[
            pl.BlockSpec((TM_PROJ, d), lambda i: (i, 0)),
            _resident((1, d)),
            _resident((d, n)),
            _resident((d, LANES)),
            _resident((1, LANES)),
        ],
        out_specs=[
            pl.BlockSpec((TM_PROJ, n), lambda i: (i, 0)),
            pl.BlockSpec((TM_PROJ, LANES), lambda i: (i, 0)),
            pl.BlockSpec((kb_per_tile, N_HEADS, TK_FOX), lambda i: (i, 0, 0)),
        ],
        out_shape=[
            jax.ShapeDtypeStruct((m, n), BF16),
            jax.ShapeDtypeStruct((m, LANES), F32),
            jax.ShapeDtypeStruct((m // TK_FOX, N_HEADS, TK_FOX), F32),
        ],
        scratch_shapes=[pltpu.VMEM((1, LANES), F32)],
        compiler_params=pltpu.CompilerParams(
            dimension_semantics=("arbitrary",), vmem_limit_bytes=VMEM_LIMIT_BYTES),
        name="fox_proj",
    )(x, g.reshape(1, d), w, wf, bf)


def _t5_bucket_table():
    qi = np.arange(WINDOW)[:, None]
    kj = np.arange(2 * WINDOW)[None, :]
    dist = qi + WINDOW - kj
    max_exact = N_BUCKETS // 2
    d = np.maximum(dist, 0)
    dl = np.maximum(d, 1).astype(np.float32)
    large = max_exact + (np.log(dl / max_exact) / math.log(MAX_DISTANCE / max_exact)
                         * (N_BUCKETS - max_exact)).astype(np.int32)
    large = np.minimum(large, N_BUCKETS - 1)
    bucket = np.where(d < max_exact, d, large)
    in_window = (dist >= 0) & (dist < WINDOW)
    return np.where(in_window, bucket, -1).astype(np.int32)


def _swa_table_kernel(rel_ref, bucket_ref, o_ref):
    h = pl.program_id(0)
    bucket = bucket_ref[...]
    bias = jnp.zeros(bucket.shape, F32)
    for b in range(N_BUCKETS):
        bias = jnp.where(bucket == b, rel_ref[b, h], bias)
    banded = jnp.where(bucket >= 0, bias, NEG)
    col = lax.broadcasted_iota(jnp.int32, bucket.shape, 1)
    o_ref[0, 0] = jnp.where(col >= WINDOW, banded, NEG)
    o_ref[1, 0] = banded


def _swa_table(rel_bias):
    bucket = jnp.asarray(_t5_bucket_table())
    return pl.pallas_call(
        _swa_table_kernel,
        grid=(N_HEADS,),
        in_specs=[
            pl.BlockSpec(memory_space=pltpu.SMEM),
            pl.BlockSpec((WINDOW, 2 * WINDOW), lambda h: (0, 0)),
        ],
        out_specs=pl.BlockSpec((2, 1, WINDOW, 2 * WINDOW), lambda h: (0, h, 0, 0)),
        out_shape=jax.ShapeDtypeStruct((2, N_HEADS, WINDOW, 2 * WINDOW), F32),
        compiler_params=pltpu.CompilerParams(dimension_semantics=("arbitrary",)),
        name="swa_table",
    )(rel_bias, bucket)


def _swa_attn_kernel(sink_ref, q_ref, kp_ref, kc_ref, vp_ref, vc_ref, tab_ref, o_ref):
    q = q_ref[...]
    k_all = jnp.concatenate([kp_ref[...], kc_ref[...]], axis=0)
    v_all = jnp.concatenate([vp_ref[...], vc_ref[...]], axis=0)
    lo = lax.broadcasted_iota(jnp.int32, (1, LANES), 1) < HEAD_DIM
    blk = lax.broadcasted_iota(jnp.int32, (4 * WINDOW, 1), 0) // WINDOW
    zero = jnp.zeros((), BF16)
    for h in range(N_KV_HEADS_A):
        kd = k_all[:, h * LANES:(h + 1) * LANES]
        vd = v_all[:, h * LANES:(h + 1) * LANES]
        q2 = jnp.concatenate([q[:, (2 * h) * LANES:(2 * h + 1) * LANES],
                              q[:, (2 * h + 1) * LANES:(2 * h + 2) * LANES]], axis=0)
        q2 = q2 * jnp.asarray(SCALE, BF16)
        heads = (4 * h, 4 * h + 2, 4 * h + 1, 4 * h + 3)
        ql = jnp.concatenate([jnp.where(lo, q2, zero), jnp.where(lo, zero, q2)], axis=0)
        s = lax.dot_general(ql, kd, (((1,), (1,)), ((), ())), preferred_element_type=F32)
        z = s + jnp.concatenate([tab_ref[0, hd] for hd in heads], axis=0)
        sink = jnp.full((4 * WINDOW, 1), sink_ref[0, heads[3]], F32)
        for r in range(3):
            sink = jnp.where(blk == r, sink_ref[0, heads[r]], sink)
        m = jnp.maximum(jnp.max(z, axis=1, keepdims=True), sink)
        p = jnp.exp(z - m)
        den = jnp.sum(p, axis=1, keepdims=True) + jnp.exp(sink - m)
        pb = (p * (1.0 / den)).astype(BF16)
        o = jnp.dot(pb, vd, preferred_element_type=F32)
        o_ref[:, (2 * h) * LANES:(2 * h + 1) * LANES] = jnp.where(
            lo, o[0:WINDOW], o[2 * WINDOW:3 * WINDOW]).astype(o_ref.dtype)
        o_ref[:, (2 * h + 1) * LANES:(2 * h + 2) * LANES] = jnp.where(
            lo, o[WINDOW:2 * WINDOW], o[3 * WINDOW:4 * WINDOW]).astype(o_ref.dtype)


def _swa_attention(qkv, table, sinks, batch, seq):
    m = qkv.shape[0]
    nb = seq // WINDOW
    kv_w = N_KV_HEADS_A * LANES
    k_col = D_MODEL // kv_w
    v_col = k_col + 1

    def cur(n, b):
        return b * nb + n

    def prev(n, b):
        return b * nb + jnp.maximum(n - 1, 0)

    return pl.pallas_call(
        _swa_attn_kernel,
        grid=(nb, batch),
        in_specs=[
            pl.BlockSpec(memory_space=pltpu.SMEM),
            pl.BlockSpec((WINDOW, D_MODEL), lambda n, b: (cur(n, b), 0)),
            pl.BlockSpec((WINDOW, kv_w), lambda n, b: (prev(n, b), k_col)),
            pl.BlockSpec((WINDOW, kv_w), lambda n, b: (cur(n, b), k_col)),
            pl.BlockSpec((WINDOW, kv_w), lambda n, b: (prev(n, b), v_col)),
            pl.BlockSpec((WINDOW, kv_w), lambda n, b: (cur(n, b), v_col)),
            pl.BlockSpec((1, N_HEADS, WINDOW, 2 * WINDOW),
                         lambda n, b: (jnp.minimum(n, 1), 0, 0, 0)),
        ],
        out_specs=pl.BlockSpec((WINDOW, D_MODEL), lambda n, b: (cur(n, b), 0)),
        out_shape=jax.ShapeDtypeStruct((m, D_MODEL), BF16),
        compiler_params=pltpu.CompilerParams(
            dimension_semantics=("arbitrary", "arbitrary"), vmem_limit_bytes=VMEM_LIMIT_BYTES),
        name="swa_attn",
    )(sinks.reshape(1, N_HEADS), qkv, qkv, qkv, qkv, qkv, table)


def _fox_attn_kernel(q_ref, k_ref, v_ref, c_ref, ct_ref, o_ref):
    pair = pl.program_id(1)
    qi = pl.program_id(2)
    tq, tk = TQ_FOX, TK_FOX
    lane = lax.broadcasted_iota(jnp.int32, (1, LANES), 1)
    lo = lane < HEAD_DIM
    zero = jnp.zeros((), BF16)
    q = q_ref[...] * jnp.asarray(SCALE, BF16)
    q_heads = (jnp.where(lo, q, zero), jnp.where(lo, zero, q))
    c_tile = c_ref[...]
    heads = (2 * pair, 2 * pair + 1)
    cq = tuple(jnp.sum(jnp.where(lane == hd, c_tile, 0.0), axis=1, keepdims=True) for hd in heads)

    def step(kb, carry, masked):
        start = pl.multiple_of(kb * tk, tk)
        k = k_ref[pl.ds(start, tk), :]
        v = v_ref[pl.ds(start, tk), :]
        new = []
        for e in range(2):
            m_prev, l_prev, acc_prev = carry[e]
            ck = ct_ref[kb, pl.ds(heads[e], 1), :]
            s = lax.dot_general(q_heads[e], k, (((1,), (1,)), ((), ())),
                                preferred_element_type=F32)
            z = s + cq[e] - ck
            if masked:
                r = lax.broadcasted_iota(jnp.int32, (tq, tk), 0)
                c = lax.broadcasted_iota(jnp.int32, (tq, tk), 1)
                z = jnp.where(r >= c, z, NEG)
            m_new = jnp.maximum(m_prev, jnp.max(z, axis=1, keepdims=True))
            alpha = jnp.exp(m_prev - m_new)
            p = jnp.exp(z - m_new)
            l_new = alpha * l_prev + jnp.sum(p, axis=1, keepdims=True)
            acc_new = alpha * acc_prev + jnp.dot(p.astype(BF16), v, preferred_element_type=F32)
            new.append((m_new, l_new, acc_new))
        return tuple(new)

    init = tuple((jnp.full((tq, 1), NEG, F32), jnp.zeros((tq, 1), F32),
                  jnp.zeros((tq, LANES), F32)) for _ in range(2))
    carry = lax.fori_loop(0, qi, lambda kb, c: step(kb, c, False), init)
    (_, l_e, acc_e), (_, l_o, acc_o) = step(qi, carry, True)
    o_ref[...] = jnp.where(lo, acc_e * (1.0 / l_e), acc_o * (1.0 / l_o)).astype(o_ref.dtype)


def _fox_attention(qkv, c, ct, batch, seq):
    m = qkv.shape[0]
    nq = seq // TQ_FOX
    nkb = seq // TK_FOX
    k_col = N_PAIRS
    v_col = 2 * N_PAIRS
    return pl.pallas_call(
        _fox_attn_kernel,
        grid=(batch, N_PAIRS, nq),
        in_specs=[
            pl.BlockSpec((TQ_FOX, LANES), lambda b, p, i: (b * nq + i, p)),
            pl.BlockSpec((seq, LANES), lambda b, p, i: (b, k_col + p)),
            pl.BlockSpec((seq, LANES), lambda b, p, i: (b, v_col + p)),
            pl.BlockSpec((TQ_FOX, LANES), lambda b, p, i: (b * nq + i, 0)),
            pl.BlockSpec((nkb, N_HEADS, TK_FOX), lambda b, p, i: (b, 0, 0)),
        ],
        out_specs=pl.BlockSpec((TQ_FOX, LANES), lambda b, p, i: (b * nq + i, p)),
        out_shape=jax.ShapeDtypeStruct((m, D_MODEL), BF16),
        compiler_params=pltpu.CompilerParams(
            dimension_semantics=("arbitrary", "arbitrary", "arbitrary"),
            vmem_limit_bytes=VMEM_LIMIT_BYTES),
        name="fox_attn",
    )(qkv, qkv, qkv, c, ct)


def _post_kernel(has_bias, has_final, *refs):
    x_ref, a_ref, wo_ref = refs[:3]
    rest = list(refs[3:])
    bo_ref = rest.pop(0) if has_bias else None
    g_ref, wup_ref, wdown_ref = rest[:3]
    rest = rest[3:]
    gf_ref = rest.pop(0) if has_final else None
    o_ref = rest[0]

    y = x_ref[...] + jnp.dot(a_ref[...], wo_ref[...], preferred_element_type=F32)
    if has_bias:
        y = y + bo_ref[...]
    h = _rmsnorm(y, g_ref[...]).astype(BF16)
    acc = y
    for c in range(D_FF // FF_CHUNK):
        sl = slice(c * FF_CHUNK, (c + 1) * FF_CHUNK)
        u = jnp.dot(h, wup_ref[:, sl], preferred_element_type=F32)
        u = jnp.square(jnp.maximum(u, 0.0)).astype(BF16)
        acc = acc + jnp.dot(u, wdown_ref[sl, :], preferred_element_type=F32)
    if has_final:
        acc = _rmsnorm(acc, gf_ref[...])
    o_ref[...] = acc


def _post(x, a, wo, bo, g, wup, wdown, gf):
    m, d = x.shape
    row_spec = pl.BlockSpec((TM_POST, d), lambda i: (i, 0))
    args = [x, a, wo]
    specs = [row_spec, row_spec, _resident((d, d))]
    if bo is not None:
        args.append(bo.reshape(1, d))
        specs.append(_resident((1, d)))
    args += [g.reshape(1, d), wup, wdown]
    specs += [_resident((1, d)), _resident((d, D_FF)), _resident((D_FF, d))]
    if gf is not None:
        args.append(gf.reshape(1, d))
        specs.append(_resident((1, d)))
    return pl.pallas_call(
        functools.partial(_post_kernel, bo is not None, gf is not None),
        grid=(m // TM_POST,),
        in_specs=specs,
        out_specs=row_spec,
        out_shape=jax.ShapeDtypeStruct((m, d), F32),
        compiler_params=pltpu.CompilerParams(
            dimension_semantics=("arbitrary",), vmem_limit_bytes=VMEM_LIMIT_BYTES),
        name="post_final" if gf is not None else ("post_bias" if bo is not None else "post"),
    )(*args)


def _dup_kv_columns(w):
    lead = w.shape[:-1]
    w = w.reshape(lead + (N_KV_HEADS_A, 1, HEAD_DIM))
    w = jnp.broadcast_to(w, lead + (N_KV_HEADS_A, 2, HEAD_DIM))
    return w.reshape(lead + (N_KV_HEADS_A * LANES,))


def _swa_weights(w_qkv, b_qkv):
    nq = N_HEADS * HEAD_DIM
    nkv = N_KV_HEADS_A * HEAD_DIM

    def relayout(t):
        return jnp.concatenate([t[..., :nq], _dup_kv_columns(t[..., nq:nq + nkv]),
                                _dup_kv_columns(t[..., nq + nkv:])], axis=-1)

    return relayout(w_qkv).astype(BF16), relayout(b_qkv)


def kernel(x, rel_bias, norm_mix, norm_mlp, w_qkv_a, b_qkv_a, sinks_a, w_o_a, b_o_a,
           w_qkvf_b, b_f_b, w_o_b, w_up, w_down, norm_final):
    batch, seq, d = x.shape
    depth = norm_mix.shape[0]
    assert d == D_MODEL and seq % TM_PROJ == 0 and (batch * seq) % TM_POST == 0
    xf = x.reshape(batch * seq, d)
    table = _swa_table(rel_bias)
    hd = N_HEADS * HEAD_DIM
    for i in range(depth):
        j = i // 2
        gf = norm_final if i == depth - 1 else None
        if i % 2 == 0:
            w, b = _swa_weights(w_qkv_a[j], b_qkv_a[j])
            qkv = _swa_proj(xf, norm_mix[i], w, b)
            a = _swa_attention(qkv, table, sinks_a[j], batch, seq)
            wo, bo = w_o_a[j].astype(BF16), b_o_a[j]
        else:
            w = w_qkvf_b[j, :, :3 * hd].astype(BF16)
            wf = jnp.pad(w_qkvf_b[j, :, 3 * hd:], ((0, 0), (0, LANES - N_HEADS))).astype(BF16)
            bf = jnp.pad(b_f_b[j], (0, LANES - N_HEADS)).reshape(1, LANES)
            qkv, c, ct = _fox_proj(xf, norm_mix[i], w, wf, bf, seq)
            a = _fox_attention(qkv, c, ct, batch, seq)
            wo, bo = w_o_b[j].astype(BF16), None
        xf = _post(xf, a, wo, bo, norm_mlp[i], w_up[i].astype(BF16), w_down[i].astype(BF16), gf)
    return xf.reshape(batch, seq, d)
```

```python
import functools
import math

import numpy as np
import jax
import jax.numpy as jnp
from jax import lax
from jax.experimental import pallas as pl
from jax.experimental.pallas import tpu as pltpu

D_MODEL = 1024
HEAD_DIM = 64
N_HEADS = 16
N_KV_HEADS_A = 4
GROUP_A = N_HEADS // N_KV_HEADS_A
WINDOW = 128
D_FF = 4 * D_MODEL
N_BUCKETS = 32
MAX_DISTANCE = 128
EPS = 1e-6
NEG = -1e30
SCALE = 1.0 / math.sqrt(HEAD_DIM)
LOG2E = math.log2(math.e)

LANES = 128
N_PAIRS = N_HEADS // 2
VMEM_LIMIT_BYTES = 56 * 1024 * 1024

TM_PROJ = 512
TM_POST = 512
FF_CHUNK = 1024
TQ_FOX = 256
FOX_Q_TILES = 2048 // TQ_FOX
FOX_PAIRS_PER_STEP = 2

BF16 = jnp.bfloat16
F32 = jnp.float32


def _resident(shape):
    zeros = (0,) * len(shape)
    return pl.BlockSpec(shape, lambda *_: zeros, pipeline_mode=pl.Buffered(1))


def _rmsnorm(x, g):
    return x * lax.rsqrt(jnp.mean(x * x, axis=-1, keepdims=True) + EPS) * g


def _proj_chunks(h, w_ref, b_ref, o_ref, n_out, chunk, scaled_cols=0, col_scale=1.0):
    for c in range(n_out // chunk):
        sl = slice(c * chunk, (c + 1) * chunk)
        acc = jnp.dot(h, w_ref[:, sl], preferred_element_type=F32)
        if b_ref is not None:
            acc = acc + b_ref[:, sl]
        if (c + 1) * chunk <= scaled_cols:
            acc = acc * col_scale
        o_ref[:, sl] = acc.astype(o_ref.dtype)


def _swa_proj_kernel(x_ref, g_ref, w_ref, b_ref, o_ref):
    h = _rmsnorm(x_ref[...], g_ref[...]).astype(BF16)
    _proj_chunks(h, w_ref, b_ref, o_ref, o_ref.shape[1], 512)


def _swa_proj(x, g, w, b):
    m, d = x.shape
    n = w.shape[1]
    return pl.pallas_call(
        _swa_proj_kernel,
        grid=(m // TM_PROJ,),
        in_specs=[
            pl.BlockSpec((TM_PROJ, d), lambda i: (i, 0)),
            _resident((1, d)),
            _resident((d, n)),
            _resident((1, n)),
        ],
        out_specs=pl.BlockSpec((TM_PROJ, n), lambda i: (i, 0)),
        out_shape=jax.ShapeDtypeStruct((m, n), BF16),
        compiler_params=pltpu.CompilerParams(
            dimension_semantics=("arbitrary",), vmem_limit_bytes=VMEM_LIMIT_BYTES),
        name="swa_proj",
    )(x, g.reshape(1, d), w, b.reshape(1, n))


def _fox_proj_kernel(tiles_per_seq, x_ref, g_ref, w_ref, wf_ref, bf_ref,
                     o_ref, c_ref, ct_ref, carry_ref):
    i = pl.program_id(0)

    @pl.when(i % tiles_per_seq == 0)
    def _():
        carry_ref[...] = jnp.zeros_like(carry_ref)

    h = _rmsnorm(x_ref[...], g_ref[...]).astype(BF16)
    _proj_chunks(h, w_ref, None, o_ref, o_ref.shape[1], 512,
                 scaled_cols=N_HEADS * HEAD_DIM, col_scale=SCALE * LOG2E)

    fz = jnp.dot(h, wf_ref[...], preferred_element_type=F32) + bf_ref[...]
    cs = (jnp.minimum(fz, 0.0) - jnp.log1p(jnp.exp(-jnp.abs(fz)))) * LOG2E
    tm = cs.shape[0]
    row = lax.broadcasted_iota(jnp.int32, cs.shape, 0)
    shift = 1
    while shift < tm:
        cs = cs + jnp.where(row >= shift, pltpu.roll(cs, shift, axis=0), 0.0)
        shift *= 2
    cs = cs + carry_ref[...]
    carry_ref[...] = cs[tm - 1:tm, :]
    c_ref[...] = cs
    ct_ref[...] = cs.T[:N_HEADS, :]


def _fox_proj(x, g, w, wf, bf, seq):
    m, d = x.shape
    n = w.shape[1]
    return pl.pallas_call(
        functools.partial(_fox_proj_kernel, seq // TM_PROJ),
        grid=(m // TM_PROJ,),
        in_specs=[
            pl.BlockSpec((TM_PROJ, d), lambda i: (i, 0)),
            _resident((1, d)),
            _resident((d, n)),
            _resident((d, LANES)),
            _resident((1, LANES)),
        ],
        out_specs=[
            pl.BlockSpec((TM_PROJ, n), lambda i: (i, 0)),
            pl.BlockSpec((TM_PROJ, LANES), lambda i: (i, 0)),
            pl.BlockSpec((N_HEADS, TM_PROJ), lambda i: (0, i)),
        ],
        out_shape=[
            jax.ShapeDtypeStruct((m, n), BF16),
            jax.ShapeDtypeStruct((m, LANES), F32),
            jax.ShapeDtypeStruct((N_HEADS, m), F32),
        ],
        scratch_shapes=[pltpu.VMEM((1, LANES), F32)],
        compiler_params=pltpu.CompilerParams(
            dimension_semantics=("arbitrary",), vmem_limit_bytes=VMEM_LIMIT_BYTES),
        name="fox_proj",
    )(x, g.reshape(1, d), w, wf, bf)


def _t5_bucket_table():
    qi = np.arange(WINDOW)[:, None]
    kj = np.arange(2 * WINDOW)[None, :]
    dist = qi + WINDOW - kj
    max_exact = N_BUCKETS // 2
    d = np.maximum(dist, 0)
    dl = np.maximum(d, 1).astype(np.float32)
    large = max_exact + (np.log(dl / max_exact) / math.log(MAX_DISTANCE / max_exact)
                         * (N_BUCKETS - max_exact)).astype(np.int32)
    large = np.minimum(large, N_BUCKETS - 1)
    bucket = np.where(d < max_exact, d, large)
    in_window = (dist >= 0) & (dist < WINDOW)
    return np.where(in_window, bucket, -1).astype(np.int32)


def _swa_table_kernel(rel_ref, bucket_ref, o_ref):
    h = pl.program_id(0)
    bucket = bucket_ref[...]
    bias = jnp.zeros(bucket.shape, F32)
    for b in range(N_BUCKETS):
        bias = jnp.where(bucket == b, rel_ref[b, h], bias)
    banded = jnp.where(bucket >= 0, bias, NEG)
    col = lax.broadcasted_iota(jnp.int32, bucket.shape, 1)
    o_ref[0, 0] = jnp.where(col >= WINDOW, banded, NEG)
    o_ref[1, 0] = banded


def _swa_table(rel_bias):
    bucket = jnp.asarray(_t5_bucket_table())
    return pl.pallas_call(
        _swa_table_kernel,
        grid=(N_HEADS,),
        in_specs=[
            pl.BlockSpec(memory_space=pltpu.SMEM),
            pl.BlockSpec((WINDOW, 2 * WINDOW), lambda h: (0, 0)),
        ],
        out_specs=pl.BlockSpec((2, 1, WINDOW, 2 * WINDOW), lambda h: (0, h, 0, 0)),
        out_shape=jax.ShapeDtypeStruct((2, N_HEADS, WINDOW, 2 * WINDOW), F32),
        compiler_params=pltpu.CompilerParams(dimension_semantics=("arbitrary",)),
        name="swa_table",
    )(rel_bias, bucket)


def _swa_attn_kernel(sink_ref, q_ref, kp_ref, kc_ref, vp_ref, vc_ref, tab_ref, o_ref):
    q = q_ref[...]
    k_all = jnp.concatenate([kp_ref[...], kc_ref[...]], axis=0)
    v_all = jnp.concatenate([vp_ref[...], vc_ref[...]], axis=0)
    lo = lax.broadcasted_iota(jnp.int32, (1, LANES), 1) < HEAD_DIM
    blk = lax.broadcasted_iota(jnp.int32, (4 * WINDOW, 1), 0) // WINDOW
    zero = jnp.zeros((), BF16)
    for h in range(N_KV_HEADS_A):
        kd = k_all[:, h * LANES:(h + 1) * LANES]
        vd = v_all[:, h * LANES:(h + 1) * LANES]
        q2 = jnp.concatenate([q[:, (2 * h) * LANES:(2 * h + 1) * LANES],
                              q[:, (2 * h + 1) * LANES:(2 * h + 2) * LANES]], axis=0)
        q2 = q2 * jnp.asarray(SCALE, BF16)
        heads = (4 * h, 4 * h + 2, 4 * h + 1, 4 * h + 3)
        ql = jnp.concatenate([jnp.where(lo, q2, zero), jnp.where(lo, zero, q2)], axis=0)
        s = lax.dot_general(ql, kd, (((1,), (1,)), ((), ())), preferred_element_type=F32)
        z = s + jnp.concatenate([tab_ref[0, hd] for hd in heads], axis=0)
        sink = jnp.full((4 * WINDOW, 1), sink_ref[0, heads[3]], F32)
        for r in range(3):
            sink = jnp.where(blk == r, sink_ref[0, heads[r]], sink)
        m = jnp.maximum(jnp.max(z, axis=1, keepdims=True), sink)
        p = jnp.exp(z - m)
        den = jnp.sum(p, axis=1, keepdims=True) + jnp.exp(sink - m)
        pb = (p * (1.0 / den)).astype(BF16)
        o = jnp.dot(pb, vd, preferred_element_type=F32)
        o_ref[:, (2 * h) * LANES:(2 * h + 1) * LANES] = jnp.where(
            lo, o[0:WINDOW], o[2 * WINDOW:3 * WINDOW]).astype(o_ref.dtype)
        o_ref[:, (2 * h + 1) * LANES:(2 * h + 2) * LANES] = jnp.where(
            lo, o[WINDOW:2 * WINDOW], o[3 * WINDOW:4 * WINDOW]).astype(o_ref.dtype)


def _swa_attention(qkv, table, sinks, batch, seq):
    m = qkv.shape[0]
    nb = seq // WINDOW
    kv_w = N_KV_HEADS_A * LANES
    k_col = D_MODEL // kv_w
    v_col = k_col + 1

    def cur(n, b):
        return b * nb + n

    def prev(n, b):
        return b * nb + jnp.maximum(n - 1, 0)

    return pl.pallas_call(
        _swa_attn_kernel,
        grid=(nb, batch),
        in_specs=[
            pl.BlockSpec(memory_space=pltpu.SMEM),
            pl.BlockSpec((WINDOW, D_MODEL), lambda n, b: (cur(n, b), 0)),
            pl.BlockSpec((WINDOW, kv_w), lambda n, b: (prev(n, b), k_col)),
            pl.BlockSpec((WINDOW, kv_w), lambda n, b: (cur(n, b), k_col)),
            pl.BlockSpec((WINDOW, kv_w), lambda n, b: (prev(n, b), v_col)),
            pl.BlockSpec((WINDOW, kv_w), lambda n, b: (cur(n, b), v_col)),
            pl.BlockSpec((1, N_HEADS, WINDOW, 2 * WINDOW),
                         lambda n, b: (jnp.minimum(n, 1), 0, 0, 0)),
        ],
        out_specs=pl.BlockSpec((WINDOW, D_MODEL), lambda n, b: (cur(n, b), 0)),
        out_shape=jax.ShapeDtypeStruct((m, D_MODEL), BF16),
        compiler_params=pltpu.CompilerParams(
            dimension_semantics=("arbitrary", "arbitrary"), vmem_limit_bytes=VMEM_LIMIT_BYTES),
        name="swa_attn",
    )(sinks.reshape(1, N_HEADS), qkv, qkv, qkv, qkv, qkv, table)


def _fox_attn_kernel(q_ref, k_ref, v_ref, c_ref, ct_ref, o_ref):
    pg = pl.program_id(1)
    qi = pl.program_id(2)
    tq = TQ_FOX
    lane = lax.broadcasted_iota(jnp.int32, (1, LANES), 1)
    lo = lane < HEAD_DIM
    zero = jnp.zeros((), BF16)
    causal = (lax.broadcasted_iota(jnp.int32, (tq, tq), 0)
              >= lax.broadcasted_iota(jnp.int32, (tq, tq), 1))

    def attend(n_keys):
        c_tile = c_ref[...]
        for g in range(FOX_PAIRS_PER_STEP):
            cols = slice(g * LANES, (g + 1) * LANES)
            q = q_ref[:, cols]
            k = k_ref[0:n_keys, cols]
            v = v_ref[0:n_keys, cols]
            outs = []
            for e in range(2):
                head = 2 * (FOX_PAIRS_PER_STEP * pg + g) + e
                qh = jnp.where(lo, q, zero) if e == 0 else jnp.where(lo, zero, q)
                s = lax.dot_general(qh, k, (((1,), (1,)), ((), ())), preferred_element_type=F32)
                u = s - ct_ref[pl.ds(head, 1), 0:n_keys]
                diag = jnp.where(causal, u[:, n_keys - tq:], NEG)
                u = diag if n_keys == tq else jnp.concatenate([u[:, :n_keys - tq], diag], axis=1)
                cq = jnp.sum(jnp.where(lane == head, c_tile, 0.0), axis=1, keepdims=True)
                z_max = jnp.max(u, axis=1, keepdims=True) + cq
                p = jnp.exp2(u + (cq - z_max))
                l = jnp.sum(p, axis=1, keepdims=True)
                o = jnp.dot(p.astype(BF16), v, preferred_element_type=F32)
                outs.append(o * (1.0 / l))
            o_ref[:, cols] = jnp.where(lo, outs[0], outs[1]).astype(o_ref.dtype)

    for i in range(FOX_Q_TILES):
        pl.when(qi == i)(functools.partial(attend, (i + 1) * tq))


def _fox_attention(qkv, c, ct, batch, seq):
    m = qkv.shape[0]
    nq = seq // TQ_FOX
    assert nq == FOX_Q_TILES
    width = FOX_PAIRS_PER_STEP * LANES
    k_col = D_MODEL // width
    v_col = 2 * k_col
    return pl.pallas_call(
        _fox_attn_kernel,
        grid=(batch, N_PAIRS // FOX_PAIRS_PER_STEP, nq),
        in_specs=[
            pl.BlockSpec((TQ_FOX, width), lambda b, p, i: (b * nq + i, p)),
            pl.BlockSpec((seq, width), lambda b, p, i: (b, k_col + p)),
            pl.BlockSpec((seq, width), lambda b, p, i: (b, v_col + p)),
            pl.BlockSpec((TQ_FOX, LANES), lambda b, p, i: (b * nq + i, 0)),
            pl.BlockSpec((N_HEADS, seq), lambda b, p, i: (0, b)),
        ],
        out_specs=pl.BlockSpec((TQ_FOX, width), lambda b, p, i: (b * nq + i, p)),
        out_shape=jax.ShapeDtypeStruct((m, D_MODEL), BF16),
        compiler_params=pltpu.CompilerParams(
            dimension_semantics=("arbitrary", "arbitrary", "arbitrary"),
            vmem_limit_bytes=VMEM_LIMIT_BYTES),
        name="fox_attn",
    )(qkv, qkv, qkv, c, ct)


def _post_kernel(has_bias, has_final, *refs):
    x_ref, a_ref, wo_ref = refs[:3]
    rest = list(refs[3:])
    bo_ref = rest.pop(0) if has_bias else None
    g_ref, wup_ref, wdown_ref = rest[:3]
    rest = rest[3:]
    gf_ref = rest.pop(0) if has_final else None
    o_ref = rest[0]

    y = x_ref[...] + jnp.dot(a_ref[...], wo_ref[...], preferred_element_type=F32)
    if has_bias:
        y = y + bo_ref[...]
    h = _rmsnorm(y, g_ref[...]).astype(BF16)
    acc = y
    for c in range(D_FF // FF_CHUNK):
        sl = slice(c * FF_CHUNK, (c + 1) * FF_CHUNK)
        u = jnp.dot(h, wup_ref[:, sl], preferred_element_type=F32)
        u = jnp.square(jnp.maximum(u, 0.0)).astype(BF16)
        acc = acc + jnp.dot(u, wdown_ref[sl, :], preferred_element_type=F32)
    if has_final:
        acc = _rmsnorm(acc, gf_ref[...])
    o_ref[...] = acc


def _post(x, a, wo, bo, g, wup, wdown, gf):
    m, d = x.shape
    row_spec = pl.BlockSpec((TM_POST, d), lambda i: (i, 0))
    args = [x, a, wo]
    specs = [row_spec, row_spec, _resident((d, d))]
    if bo is not None:
        args.append(bo.reshape(1, d))
        specs.append(_resident((1, d)))
    args += [g.reshape(1, d), wup, wdown]
    specs += [_resident((1, d)), _resident((d, D_FF)), _resident((D_FF, d))]
    if gf is not None:
        args.append(gf.reshape(1, d))
        specs.append(_resident((1, d)))
    return pl.pallas_call(
        functools.partial(_post_kernel, bo is not None, gf is not None),
        grid=(m // TM_POST,),
        in_specs=specs,
        out_specs=row_spec,
        out_shape=jax.ShapeDtypeStruct((m, d), F32),
        compiler_params=pltpu.CompilerParams(
            dimension_semantics=("arbitrary",), vmem_limit_bytes=VMEM_LIMIT_BYTES),
        name="post_final" if gf is not None else ("post_bias" if bo is not None else "post"),
    )(*args)


def _dup_kv_columns(w):
    lead = w.shape[:-1]
    w = w.reshape(lead + (N_KV_HEADS_A, 1, HEAD_DIM))
    w = jnp.broadcast_to(w, lead + (N_KV_HEADS_A, 2, HEAD_DIM))
    return w.reshape(lead + (N_KV_HEADS_A * LANES,))


def _swa_weights(w_qkv, b_qkv):
    nq = N_HEADS * HEAD_DIM
    nkv = N_KV_HEADS_A * HEAD_DIM

    def relayout(t):
        return jnp.concatenate([t[..., :nq], _dup_kv_columns(t[..., nq:nq + nkv]),
                                _dup_kv_columns(t[..., nq + nkv:])], axis=-1)

    return relayout(w_qkv).astype(BF16), relayout(b_qkv)


def kernel(x, rel_bias, norm_mix, norm_mlp, w_qkv_a, b_qkv_a, sinks_a, w_o_a, b_o_a,
           w_qkvf_b, b_f_b, w_o_b, w_up, w_down, norm_final):
    batch, seq, d = x.shape
    depth = norm_mix.shape[0]
    assert d == D_MODEL and seq % TM_PROJ == 0 and (batch * seq) % TM_POST == 0
    xf = x.reshape(batch * seq, d)
    table = _swa_table(rel_bias)
    hd = N_HEADS * HEAD_DIM
    for i in range(depth):
        j = i // 2
        gf = norm_final if i == depth - 1 else None
        if i % 2 == 0:
            w, b = _swa_weights(w_qkv_a[j], b_qkv_a[j])
            qkv = _swa_proj(xf, norm_mix[i], w, b)
            a = _swa_attention(qkv, table, sinks_a[j], batch, seq)
            wo, bo = w_o_a[j].astype(BF16), b_o_a[j]
        else:
            w = w_qkvf_b[j, :, :3 * hd].astype(BF16)
            wf = jnp.pad(w_qkvf_b[j, :, 3 * hd:], ((0, 0), (0, LANES - N_HEADS))).astype(BF16)
            bf = jnp.pad(b_f_b[j], (0, LANES - N_HEADS)).reshape(1, LANES)
            qkv, c, ct = _fox_proj(xf, norm_mix[i], w, wf, bf, seq)
            a = _fox_attention(qkv, c, ct, batch, seq)
            wo, bo = w_o_b[j].astype(BF16), None
        xf = _post(xf, a, wo, bo, norm_mlp[i], w_up[i].astype(BF16), w_down[i].astype(BF16), gf)
    return xf.reshape(batch, seq, d)
```

```python
import functools
import math

import numpy as np
import jax
import jax.numpy as jnp
from jax import lax
from jax.experimental import pallas as pl
from jax.experimental.pallas import tpu as pltpu

D_MODEL = 1024
HEAD_DIM = 64
N_HEADS = 16
N_KV_HEADS_A = 4
GROUP_A = N_HEADS // N_KV_HEADS_A
WINDOW = 128
D_FF = 4 * D_MODEL
N_BUCKETS = 32
MAX_DISTANCE = 128
EPS = 1e-6
NEG = -1e30
SCALE = 1.0 / math.sqrt(HEAD_DIM)
LOG2E = math.log2(math.e)

LANES = 128
N_PAIRS = N_HEADS // 2
VMEM_LIMIT_BYTES = 56 * 1024 * 1024

TM_PROJ = 512
TM_POST = 512
FF_CHUNK = 1024
TQ_FOX = 256
FOX_Q_TILES = 2048 // TQ_FOX
FOX_PAIRS_PER_STEP = 4

BF16 = jnp.bfloat16
F32 = jnp.float32


def _resident(shape):
    zeros = (0,) * len(shape)
    return pl.BlockSpec(shape, lambda *_: zeros, pipeline_mode=pl.Buffered(1))


def _rmsnorm(x, g):
    return x * lax.rsqrt(jnp.mean(x * x, axis=-1, keepdims=True) + EPS) * g


def _proj_chunks(h, w_ref, b_ref, o_ref, n_out, chunk, scaled_cols=0, col_scale=1.0):
    for c in range(n_out // chunk):
        sl = slice(c * chunk, (c + 1) * chunk)
        acc = jnp.dot(h, w_ref[:, sl], preferred_element_type=F32)
        if b_ref is not None:
            acc = acc + b_ref[:, sl]
        if (c + 1) * chunk <= scaled_cols:
            acc = acc * col_scale
        o_ref[:, sl] = acc.astype(o_ref.dtype)


def _swa_proj_kernel(x_ref, g_ref, w_ref, b_ref, o_ref):
    h = _rmsnorm(x_ref[...], g_ref[...]).astype(BF16)
    _proj_chunks(h, w_ref, b_ref, o_ref, o_ref.shape[1], 512)


def _swa_proj(x, g, w, b):
    m, d = x.shape
    n = w.shape[1]
    return pl.pallas_call(
        _swa_proj_kernel,
        grid=(m // TM_PROJ,),
        in_specs=[
            pl.BlockSpec((TM_PROJ, d), lambda i: (i, 0)),
            _resident((1, d)),
            _resident((d, n)),
            _resident((1, n)),
        ],
        out_specs=pl.BlockSpec((TM_PROJ, n), lambda i: (i, 0)),
        out_shape=jax.ShapeDtypeStruct((m, n), BF16),
        compiler_params=pltpu.CompilerParams(
            dimension_semantics=("arbitrary",), vmem_limit_bytes=VMEM_LIMIT_BYTES),
        name="swa_proj",
    )(x, g.reshape(1, d), w, b.reshape(1, n))


def _fox_proj_kernel(tiles_per_seq, x_ref, g_ref, w_ref, wf_ref, bf_ref,
                     o_ref, c_ref, ct_ref, carry_ref):
    i = pl.program_id(0)

    @pl.when(i % tiles_per_seq == 0)
    def _():
        carry_ref[...] = jnp.zeros_like(carry_ref)

    h = _rmsnorm(x_ref[...], g_ref[...]).astype(BF16)
    _proj_chunks(h, w_ref, None, o_ref, o_ref.shape[1], 512,
                 scaled_cols=N_HEADS * HEAD_DIM, col_scale=SCALE * LOG2E)

    fz = jnp.dot(h, wf_ref[...], preferred_element_type=F32) + bf_ref[...]
    cs = (jnp.minimum(fz, 0.0) - jnp.log1p(jnp.exp(-jnp.abs(fz)))) * LOG2E
    tm = cs.shape[0]
    row = lax.broadcasted_iota(jnp.int32, cs.shape, 0)
    shift = 1
    while shift < tm:
        cs = cs + jnp.where(row >= shift, pltpu.roll(cs, shift, axis=0), 0.0)
        shift *= 2
    cs = cs + carry_ref[...]
    carry_ref[...] = cs[tm - 1:tm, :]
    c_ref[...] = cs
    ct_ref[...] = cs.T[:N_HEADS, :]


def _fox_proj(x, g, w, wf, bf, seq):
    m, d = x.shape
    n = w.shape[1]
    return pl.pallas_call(
        functools.partial(_fox_proj_kernel, seq // TM_PROJ),
        grid=(m // TM_PROJ,),
        in_specs=[
            pl.BlockSpec((TM_PROJ, d), lambda i: (i, 0)),
            _resident((1, d)),
            _resident((d, n)),
            _resident((d, LANES)),
            _resident((1, LANES)),
        ],
        out_specs=[
            pl.BlockSpec((TM_PROJ, n), lambda i: (i, 0)),
            pl.BlockSpec((TM_PROJ, LANES), lambda i: (i, 0)),
            pl.BlockSpec((N_HEADS, TM_PROJ), lambda i: (0, i)),
        ],
        out_shape=[
            jax.ShapeDtypeStruct((m, n), BF16),
            jax.ShapeDtypeStruct((m, LANES), F32),
            jax.ShapeDtypeStruct((N_HEADS, m), F32),
        ],
        scratch_shapes=[pltpu.VMEM((1, LANES), F32)],
        compiler_params=pltpu.CompilerParams(
            dimension_semantics=("arbitrary",), vmem_limit_bytes=VMEM_LIMIT_BYTES),
        name="fox_proj",
    )(x, g.reshape(1, d), w, wf, bf)


def _t5_bucket_table():
    qi = np.arange(WINDOW)[:, None]
    kj = np.arange(2 * WINDOW)[None, :]
    dist = qi + WINDOW - kj
    max_exact = N_BUCKETS // 2
    d = np.maximum(dist, 0)
    dl = np.maximum(d, 1).astype(np.float32)
    large = max_exact + (np.log(dl / max_exact) / math.log(MAX_DISTANCE / max_exact)
                         * (N_BUCKETS - max_exact)).astype(np.int32)
    large = np.minimum(large, N_BUCKETS - 1)
    bucket = np.where(d < max_exact, d, large)
    in_window = (dist >= 0) & (dist < WINDOW)
    return np.where(in_window, bucket, -1).astype(np.int32)


def _swa_table_kernel(rel_ref, bucket_ref, o_ref):
    h = pl.program_id(0)
    bucket = bucket_ref[...]
    bias = jnp.zeros(bucket.shape, F32)
    for b in range(N_BUCKETS):
        bias = jnp.where(bucket == b, rel_ref[b, h], bias)
    banded = jnp.where(bucket >= 0, bias, NEG)
    col = lax.broadcasted_iota(jnp.int32, bucket.shape, 1)
    o_ref[0, 0] = jnp.where(col >= WINDOW, banded, NEG)
    o_ref[1, 0] = banded


def _swa_table(rel_bias):
    bucket = jnp.asarray(_t5_bucket_table())
    return pl.pallas_call(
        _swa_table_kernel,
        grid=(N_HEADS,),
        in_specs=[
            pl.BlockSpec(memory_space=pltpu.SMEM),
            pl.BlockSpec((WINDOW, 2 * WINDOW), lambda h: (0, 0)),
        ],
        out_specs=pl.BlockSpec((2, 1, WINDOW, 2 * WINDOW), lambda h: (0, h, 0, 0)),
        out_shape=jax.ShapeDtypeStruct((2, N_HEADS, WINDOW, 2 * WINDOW), F32),
        compiler_params=pltpu.CompilerParams(dimension_semantics=("arbitrary",)),
        name="swa_table",
    )(rel_bias, bucket)


def _swa_attn_kernel(sink_ref, q_ref, kp_ref, kc_ref, vp_ref, vc_ref, tab_ref, o_ref):
    q = q_ref[...]
    k_all = jnp.concatenate([kp_ref[...], kc_ref[...]], axis=0)
    v_all = jnp.concatenate([vp_ref[...], vc_ref[...]], axis=0)
    lo = lax.broadcasted_iota(jnp.int32, (1, LANES), 1) < HEAD_DIM
    blk = lax.broadcasted_iota(jnp.int32, (4 * WINDOW, 1), 0) // WINDOW
    zero = jnp.zeros((), BF16)
    for h in range(N_KV_HEADS_A):
        kd = k_all[:, h * LANES:(h + 1) * LANES]
        vd = v_all[:, h * LANES:(h + 1) * LANES]
        q2 = jnp.concatenate([q[:, (2 * h) * LANES:(2 * h + 1) * LANES],
                              q[:, (2 * h + 1) * LANES:(2 * h + 2) * LANES]], axis=0)
        q2 = q2 * jnp.asarray(SCALE, BF16)
        heads = (4 * h, 4 * h + 2, 4 * h + 1, 4 * h + 3)
        ql = jnp.concatenate([jnp.where(lo, q2, zero), jnp.where(lo, zero, q2)], axis=0)
        s = lax.dot_general(ql, kd, (((1,), (1,)), ((), ())), preferred_element_type=F32)
        z = s + jnp.concatenate([tab_ref[0, hd] for hd in heads], axis=0)
        sink = jnp.full((4 * WINDOW, 1), sink_ref[0, heads[3]], F32)
        for r in range(3):
            sink = jnp.where(blk == r, sink_ref[0, heads[r]], sink)
        m = jnp.maximum(jnp.max(z, axis=1, keepdims=True), sink)
        p = jnp.exp(z - m)
        den = jnp.sum(p, axis=1, keepdims=True) + jnp.exp(sink - m)
        pb = (p * (1.0 / den)).astype(BF16)
        o = jnp.dot(pb, vd, preferred_element_type=F32)
        o_ref[:, (2 * h) * LANES:(2 * h + 1) * LANES] = jnp.where(
            lo, o[0:WINDOW], o[2 * WINDOW:3 * WINDOW]).astype(o_ref.dtype)
        o_ref[:, (2 * h + 1) * LANES:(2 * h + 2) * LANES] = jnp.where(
            lo, o[WINDOW:2 * WINDOW], o[3 * WINDOW:4 * WINDOW]).astype(o_ref.dtype)


def _swa_attention(qkv, table, sinks, batch, seq):
    m = qkv.shape[0]
    nb = seq // WINDOW
    kv_w = N_KV_HEADS_A * LANES
    k_col = D_MODEL // kv_w
    v_col = k_col + 1

    def cur(n, b):
        return b * nb + n

    def prev(n, b):
        return b * nb + jnp.maximum(n - 1, 0)

    return pl.pallas_call(
        _swa_attn_kernel,
        grid=(nb, batch),
        in_specs=[
            pl.BlockSpec(memory_space=pltpu.SMEM),
            pl.BlockSpec((WINDOW, D_MODEL), lambda n, b: (cur(n, b), 0)),
            pl.BlockSpec((WINDOW, kv_w), lambda n, b: (prev(n, b), k_col)),
            pl.BlockSpec((WINDOW, kv_w), lambda n, b: (cur(n, b), k_col)),
            pl.BlockSpec((WINDOW, kv_w), lambda n, b: (prev(n, b), v_col)),
            pl.BlockSpec((WINDOW, kv_w), lambda n, b: (cur(n, b), v_col)),
            pl.BlockSpec((1, N_HEADS, WINDOW, 2 * WINDOW),
                         lambda n, b: (jnp.minimum(n, 1), 0, 0, 0)),
        ],
        out_specs=pl.BlockSpec((WINDOW, D_MODEL), lambda n, b: (cur(n, b), 0)),
        out_shape=jax.ShapeDtypeStruct((m, D_MODEL), BF16),
        compiler_params=pltpu.CompilerParams(
            dimension_semantics=("arbitrary", "arbitrary"), vmem_limit_bytes=VMEM_LIMIT_BYTES),
        name="swa_attn",
    )(sinks.reshape(1, N_HEADS), qkv, qkv, qkv, qkv, qkv, table)


def _fox_attn_kernel(q_ref, k_ref, v_ref, c_ref, ct_ref, o_ref):
    pg = pl.program_id(1)
    qi = pl.program_id(2)
    tq = TQ_FOX
    lane = lax.broadcasted_iota(jnp.int32, (1, LANES), 1)
    lo = lane < HEAD_DIM
    zero = jnp.zeros((), BF16)
    causal = (lax.broadcasted_iota(jnp.int32, (tq, tq), 0)
              >= lax.broadcasted_iota(jnp.int32, (tq, tq), 1))

    def attend(n_keys):
        c_tile = c_ref[...]
        n_heads = 2 * FOX_PAIRS_PER_STEP

        def scores(i):
            g, e = divmod(i, 2)
            q = q_ref[:, g * LANES:(g + 1) * LANES]
            qh = jnp.where(lo, q, zero) if e == 0 else jnp.where(lo, zero, q)
            k = k_ref[0:n_keys, g * LANES:(g + 1) * LANES]
            return lax.dot_general(qh, k, (((1,), (1,)), ((), ())), preferred_element_type=F32)

        s_next = scores(0)
        outs = []
        for i in range(n_heads):
            s = s_next
            if i + 1 < n_heads:
                s_next = scores(i + 1)
            g = i // 2
            head = 2 * FOX_PAIRS_PER_STEP * pg + i
            u = s - ct_ref[pl.ds(head, 1), 0:n_keys]
            diag = jnp.where(causal, u[:, n_keys - tq:], NEG)
            u = diag if n_keys == tq else jnp.concatenate([u[:, :n_keys - tq], diag], axis=1)
            cq = jnp.sum(jnp.where(lane == head, c_tile, 0.0), axis=1, keepdims=True)
            z_max = jnp.max(u, axis=1, keepdims=True) + cq
            p = jnp.exp2(u + (cq - z_max))
            l = jnp.sum(p, axis=1, keepdims=True)
            v = v_ref[0:n_keys, g * LANES:(g + 1) * LANES]
            o = jnp.dot(p.astype(BF16), v, preferred_element_type=F32)
            outs.append(o * (1.0 / l))
            if i % 2 == 1:
                o_ref[:, g * LANES:(g + 1) * LANES] = jnp.where(
                    lo, outs[i - 1], outs[i]).astype(o_ref.dtype)

    for i in range(FOX_Q_TILES):
        pl.when(qi == i)(functools.partial(attend, (i + 1) * tq))


def _fox_attention(qkv, c, ct, batch, seq):
    m = qkv.shape[0]
    nq = seq // TQ_FOX
    assert nq == FOX_Q_TILES
    width = FOX_PAIRS_PER_STEP * LANES
    k_col = D_MODEL // width
    v_col = 2 * k_col
    return pl.pallas_call(
        _fox_attn_kernel,
        grid=(batch, N_PAIRS // FOX_PAIRS_PER_STEP, nq),
        in_specs=[
            pl.BlockSpec((TQ_FOX, width), lambda b, p, i: (b * nq + i, p)),
            pl.BlockSpec((seq, width), lambda b, p, i: (b, k_col + p)),
            pl.BlockSpec((seq, width), lambda b, p, i: (b, v_col + p)),
            pl.BlockSpec((TQ_FOX, LANES), lambda b, p, i: (b * nq + i, 0)),
            pl.BlockSpec((N_HEADS, seq), lambda b, p, i: (0, b)),
        ],
        out_specs=pl.BlockSpec((TQ_FOX, width), lambda b, p, i: (b * nq + i, p)),
        out_shape=jax.ShapeDtypeStruct((m, D_MODEL), BF16),
        compiler_params=pltpu.CompilerParams(
            dimension_semantics=("arbitrary", "arbitrary", "arbitrary"),
            vmem_limit_bytes=VMEM_LIMIT_BYTES),
        name="fox_attn",
    )(qkv, qkv, qkv, c, ct)


def _post_kernel(has_bias, has_final, *refs):
    x_ref, a_ref, wo_ref = refs[:3]
    rest = list(refs[3:])
    bo_ref = rest.pop(0) if has_bias else None
    g_ref, wup_ref, wdown_ref = rest[:3]
    rest = rest[3:]
    gf_ref = rest.pop(0) if has_final else None
    o_ref = rest[0]

    y = x_ref[...] + jnp.dot(a_ref[...], wo_ref[...], preferred_element_type=F32)
    if has_bias:
        y = y + bo_ref[...]
    h = _rmsnorm(y, g_ref[...]).astype(BF16)
    acc = y
    for c in range(D_FF // FF_CHUNK):
        sl = slice(c * FF_CHUNK, (c + 1) * FF_CHUNK)
        u = jnp.dot(h, wup_ref[:, sl], preferred_element_type=F32)
        u = jnp.square(jnp.maximum(u, 0.0)).astype(BF16)
        acc = acc + jnp.dot(u, wdown_ref[sl, :], preferred_element_type=F32)
    if has_final:
        acc = _rmsnorm(acc, gf_ref[...])
    o_ref[...] = acc


def _post(x, a, wo, bo, g, wup, wdown, gf):
    m, d = x.shape
    row_spec = pl.BlockSpec((TM_POST, d), lambda i: (i, 0))
    args = [x, a, wo]
    specs = [row_spec, row_spec, _resident((d, d))]
    if bo is not None:
        args.append(bo.reshape(1, d))
        specs.append(_resident((1, d)))
    args += [g.reshape(1, d), wup, wdown]
    specs += [_resident((1, d)), _resident((d, D_FF)), _resident((D_FF, d))]
    if gf is not None:
        args.append(gf.reshape(1, d))
        specs.append(_resident((1, d)))
    return pl.pallas_call(
        functools.partial(_post_kernel, bo is not None, gf is not None),
        grid=(m // TM_POST,),
        in_specs=specs,
        out_specs=row_spec,
        out_shape=jax.ShapeDtypeStruct((m, d), F32),
        compiler_params=pltpu.CompilerParams(
            dimension_semantics=("arbitrary",), vmem_limit_bytes=VMEM_LIMIT_BYTES),
        name="post_final" if gf is not None else ("post_bias" if bo is not None else "post"),
    )(*args)


def _dup_kv_columns(w):
    lead = w.shape[:-1]
    w = w.reshape(lead + (N_KV_HEADS_A, 1, HEAD_DIM))
    w = jnp.broadcast_to(w, lead + (N_KV_HEADS_A, 2, HEAD_DIM))
    return w.reshape(lead + (N_KV_HEADS_A * LANES,))


def _swa_weights(w_qkv, b_qkv):
    nq = N_HEADS * HEAD_DIM
    nkv = N_KV_HEADS_A * HEAD_DIM

    def relayout(t):
        return jnp.concatenate([t[..., :nq], _dup_kv_columns(t[..., nq:nq + nkv]),
                                _dup_kv_columns(t[..., nq + nkv:])], axis=-1)

    return relayout(w_qkv).astype(BF16), relayout(b_qkv)


def kernel(x, rel_bias, norm_mix, norm_mlp, w_qkv_a, b_qkv_a, sinks_a, w_o_a, b_o_a,
           w_qkvf_b, b_f_b, w_o_b, w_up, w_down, norm_final):
    batch, seq, d = x.shape
    depth = norm_mix.shape[0]
    assert d == D_MODEL and seq % TM_PROJ == 0 and (batch * seq) % TM_POST == 0
    xf = x.reshape(batch * seq, d)
    table = _swa_table(rel_bias)
    hd = N_HEADS * HEAD_DIM
    for i in range(depth):
        j = i // 2
        gf = norm_final if i == depth - 1 else None
        if i % 2 == 0:
            w, b = _swa_weights(w_qkv_a[j], b_qkv_a[j])
            qkv = _swa_proj(xf, norm_mix[i], w, b)
            a = _swa_attention(qkv, table, sinks_a[j], batch, seq)
            wo, bo = w_o_a[j].astype(BF16), b_o_a[j]
        else:
            w = w_qkvf_b[j, :, :3 * hd].astype(BF16)
            wf = jnp.pad(w_qkvf_b[j, :, 3 * hd:], ((0, 0), (0, LANES - N_HEADS))).astype(BF16)
            bf = jnp.pad(b_f_b[j], (0, LANES - N_HEADS)).reshape(1, LANES)
            qkv, c, ct = _fox_proj(xf, norm_mix[i], w, wf, bf, seq)
            a = _fox_attention(qkv, c, ct, batch, seq)
            wo, bo = w_o_b[j].astype(BF16), None
        xf = _post(xf, a, wo, bo, norm_mlp[i], w_up[i].astype(BF16), w_down[i].astype(BF16), gf)
    return xf.reshape(batch, seq, d)
```

```python
import functools
import math

import numpy as np
import jax
import jax.numpy as jnp
from jax import lax
from jax.experimental import pallas as pl
from jax.experimental.pallas import tpu as pltpu

D_MODEL = 1024
HEAD_DIM = 64
N_HEADS = 16
N_KV_HEADS_A = 4
GROUP_A = N_HEADS // N_KV_HEADS_A
WINDOW = 128
D_FF = 4 * D_MODEL
N_BUCKETS = 32
MAX_DISTANCE = 128
EPS = 1e-6
NEG = -1e30
SCALE = 1.0 / math.sqrt(HEAD_DIM)
LOG2E = math.log2(math.e)

LANES = 128
N_PAIRS = N_HEADS // 2
VMEM_LIMIT_BYTES = 56 * 1024 * 1024

TM_PROJ = 512
TM_POST = 512
FF_CHUNK = 1024
TQ_FOX = 256
FOX_Q_TILES = 2048 // TQ_FOX
FOX_PAIRS_PER_STEP = 4
SWA_BLOCKS_PER_STEP = 2

BF16 = jnp.bfloat16
F32 = jnp.float32


def _resident(shape):
    zeros = (0,) * len(shape)
    return pl.BlockSpec(shape, lambda *_: zeros, pipeline_mode=pl.Buffered(1))


def _rmsnorm(x, g):
    return x * lax.rsqrt(jnp.mean(x * x, axis=-1, keepdims=True) + EPS) * g


def _proj_chunks(h, w_ref, b_ref, o_ref, n_out, chunk, scaled_cols=0, col_scale=1.0):
    for c in range(n_out // chunk):
        sl = slice(c * chunk, (c + 1) * chunk)
        acc = jnp.dot(h, w_ref[:, sl], preferred_element_type=F32)
        if b_ref is not None:
            acc = acc + b_ref[:, sl]
        if (c + 1) * chunk <= scaled_cols:
            acc = acc * col_scale
        o_ref[:, sl] = acc.astype(o_ref.dtype)


def _swa_proj_kernel(x_ref, g_ref, w_ref, b_ref, o_ref):
    h = _rmsnorm(x_ref[...], g_ref[...]).astype(BF16)
    _proj_chunks(h, w_ref, b_ref, o_ref, o_ref.shape[1], 512,
                 scaled_cols=N_HEADS * HEAD_DIM, col_scale=SCALE * LOG2E)


def _swa_proj(x, g, w, b):
    m, d = x.shape
    n = w.shape[1]
    return pl.pallas_call(
        _swa_proj_kernel,
        grid=(m // TM_PROJ,),
        in_specs=[
            pl.BlockSpec((TM_PROJ, d), lambda i: (i, 0)),
            _resident((1, d)),
            _resident((d, n)),
            _resident((1, n)),
        ],
        out_specs=pl.BlockSpec((TM_PROJ, n), lambda i: (i, 0)),
        out_shape=jax.ShapeDtypeStruct((m, n), BF16),
        compiler_params=pltpu.CompilerParams(
            dimension_semantics=("arbitrary",), vmem_limit_bytes=VMEM_LIMIT_BYTES),
        name="swa_proj",
    )(x, g.reshape(1, d), w, b.reshape(1, n))


def _fox_proj_kernel(tiles_per_seq, x_ref, g_ref, w_ref, wf_ref, bf_ref,
                     o_ref, c_ref, ct_ref, carry_ref):
    i = pl.program_id(0)

    @pl.when(i % tiles_per_seq == 0)
    def _():
        carry_ref[...] = jnp.zeros_like(carry_ref)

    h = _rmsnorm(x_ref[...], g_ref[...]).astype(BF16)
    _proj_chunks(h, w_ref, None, o_ref, o_ref.shape[1], 512,
                 scaled_cols=N_HEADS * HEAD_DIM, col_scale=SCALE * LOG2E)

    fz = jnp.dot(h, wf_ref[...], preferred_element_type=F32) + bf_ref[...]
    cs = (jnp.minimum(fz, 0.0) - jnp.log1p(jnp.exp(-jnp.abs(fz)))) * LOG2E
    tm = cs.shape[0]
    row = lax.broadcasted_iota(jnp.int32, cs.shape, 0)
    shift = 1
    while shift < tm:
        cs = cs + jnp.where(row >= shift, pltpu.roll(cs, shift, axis=0), 0.0)
        shift *= 2
    cs = cs + carry_ref[...]
    carry_ref[...] = cs[tm - 1:tm, :]
    c_ref[...] = cs
    ct_ref[...] = cs.T[:N_HEADS, :]


def _fox_proj(x, g, w, wf, bf, seq):
    m, d = x.shape
    n = w.shape[1]
    return pl.pallas_call(
        functools.partial(_fox_proj_kernel, seq // TM_PROJ),
        grid=(m // TM_PROJ,),
        in_specs=[
            pl.BlockSpec((TM_PROJ, d), lambda i: (i, 0)),
            _resident((1, d)),
            _resident((d, n)),
            _resident((d, LANES)),
            _resident((1, LANES)),
        ],
        out_specs=[
            pl.BlockSpec((TM_PROJ, n), lambda i: (i, 0)),
            pl.BlockSpec((TM_PROJ, LANES), lambda i: (i, 0)),
            pl.BlockSpec((N_HEADS, TM_PROJ), lambda i: (0, i)),
        ],
        out_shape=[
            jax.ShapeDtypeStruct((m, n), BF16),
            jax.ShapeDtypeStruct((m, LANES), F32),
            jax.ShapeDtypeStruct((N_HEADS, m), F32),
        ],
        scratch_shapes=[pltpu.VMEM((1, LANES), F32)],
        compiler_params=pltpu.CompilerParams(
            dimension_semantics=("arbitrary",), vmem_limit_bytes=VMEM_LIMIT_BYTES),
        name="fox_proj",
    )(x, g.reshape(1, d), w, wf, bf)


def _t5_bucket_table():
    qi = np.arange(WINDOW)[:, None]
    kj = np.arange(2 * WINDOW)[None, :]
    dist = qi + WINDOW - kj
    max_exact = N_BUCKETS // 2
    d = np.maximum(dist, 0)
    dl = np.maximum(d, 1).astype(np.float32)
    large = max_exact + (np.log(dl / max_exact) / math.log(MAX_DISTANCE / max_exact)
                         * (N_BUCKETS - max_exact)).astype(np.int32)
    large = np.minimum(large, N_BUCKETS - 1)
    bucket = np.where(d < max_exact, d, large)
    in_window = (dist >= 0) & (dist < WINDOW)
    return np.where(in_window, bucket, -1).astype(np.int32)


def _swa_table_kernel(rel_ref, sink_ref, bucket_ref, o_ref):
    h = pl.program_id(0)
    bucket = bucket_ref[...]
    bias = jnp.zeros(bucket.shape, F32)
    for b in range(N_BUCKETS):
        bias = jnp.where(bucket == b, rel_ref[b, h], bias)
    col = lax.broadcasted_iota(jnp.int32, bucket.shape, 1)
    banded = jnp.where(bucket >= 0, bias * LOG2E, NEG)
    first = jnp.where(col >= WINDOW, banded, NEG)
    sink = sink_ref[0, h] * LOG2E
    o_ref[0, 0] = jnp.where(col == 0, sink, first)
    o_ref[1, 0] = jnp.where(col == 0, sink, banded)


def _swa_table(rel_bias, sinks):
    bucket = _t5_bucket_table()
    assert bucket[:, 0].max() < 0
    return pl.pallas_call(
        _swa_table_kernel,
        grid=(N_HEADS,),
        in_specs=[
            pl.BlockSpec(memory_space=pltpu.SMEM),
            pl.BlockSpec(memory_space=pltpu.SMEM),
            pl.BlockSpec((WINDOW, 2 * WINDOW), lambda h: (0, 0)),
        ],
        out_specs=pl.BlockSpec((2, 1, WINDOW, 2 * WINDOW), lambda h: (0, h, 0, 0)),
        out_shape=jax.ShapeDtypeStruct((2, N_HEADS, WINDOW, 2 * WINDOW), F32),
        compiler_params=pltpu.CompilerParams(dimension_semantics=("arbitrary",)),
        name="swa_table",
    )(rel_bias, sinks.reshape(1, N_HEADS), jnp.asarray(bucket))


def _swa_attn_kernel(q_ref, kp_ref, kc_ref, vp_ref, vc_ref, tab_ref, o_ref):
    n = pl.program_id(0)
    k_all = jnp.concatenate([kp_ref[...], kc_ref[...]], axis=0)
    v_all = jnp.concatenate([vp_ref[...], vc_ref[...]], axis=0)
    lo = lax.broadcasted_iota(jnp.int32, (1, LANES), 1) < HEAD_DIM
    zero = jnp.zeros((), BF16)
    key0 = lax.broadcasted_iota(jnp.int32, (2 * WINDOW, LANES), 0) == 0
    units = [(j, h) for j in range(SWA_BLOCKS_PER_STEP) for h in range(N_KV_HEADS_A)]

    def scores(unit):
        j, h = unit
        rows = slice(j * WINDOW, (j + 1) * WINDOW)
        q2 = jnp.concatenate([q_ref[rows, (2 * h) * LANES:(2 * h + 1) * LANES],
                              q_ref[rows, (2 * h + 1) * LANES:(2 * h + 2) * LANES]], axis=0)
        ql = jnp.concatenate([jnp.where(lo, q2, zero), jnp.where(lo, zero, q2)], axis=0)
        kd = k_all[j * WINDOW:(j + 2) * WINDOW, h * LANES:(h + 1) * LANES]
        kd = jnp.where(key0, zero, kd)
        return lax.dot_general(ql, kd, (((1,), (1,)), ((), ())), preferred_element_type=F32)

    s_next = scores(units[0])
    for idx, (j, h) in enumerate(units):
        s = s_next
        if idx + 1 < len(units):
            s_next = scores(units[idx + 1])
        heads = (4 * h, 4 * h + 2, 4 * h + 1, 4 * h + 3)
        variant = jnp.minimum(n, 1) if j == 0 else 1
        z = s + jnp.concatenate([tab_ref[variant, hd] for hd in heads], axis=0)
        p = jnp.exp2(z - jnp.max(z, axis=1, keepdims=True))
        den = jnp.sum(p, axis=1, keepdims=True)
        vd = v_all[j * WINDOW:(j + 2) * WINDOW, h * LANES:(h + 1) * LANES]
        vd = jnp.where(key0, zero, vd)
        o = jnp.dot(p.astype(BF16), vd, preferred_element_type=F32) * (1.0 / den)
        rows = slice(j * WINDOW, (j + 1) * WINDOW)
        o_ref[rows, (2 * h) * LANES:(2 * h + 1) * LANES] = jnp.where(
            lo, o[0:WINDOW], o[2 * WINDOW:3 * WINDOW]).astype(o_ref.dtype)
        o_ref[rows, (2 * h + 1) * LANES:(2 * h + 2) * LANES] = jnp.where(
            lo, o[WINDOW:2 * WINDOW], o[3 * WINDOW:4 * WINDOW]).astype(o_ref.dtype)


def _swa_attention(qkv, table, batch, seq):
    m = qkv.shape[0]
    rows = SWA_BLOCKS_PER_STEP * WINDOW
    steps = seq // rows
    kv_w = N_KV_HEADS_A * LANES
    k_col = D_MODEL // kv_w
    v_col = k_col + 1

    def cur(n, b):
        return b * steps + n

    def prev(n, b):
        return jnp.maximum((b * steps + n) * SWA_BLOCKS_PER_STEP - 1, 0)

    return pl.pallas_call(
        _swa_attn_kernel,
        grid=(steps, batch),
        in_specs=[
            pl.BlockSpec((rows, D_MODEL), lambda n, b: (cur(n, b), 0)),
            pl.BlockSpec((WINDOW, kv_w), lambda n, b: (prev(n, b), k_col)),
            pl.BlockSpec((rows, kv_w), lambda n, b: (cur(n, b), k_col)),
            pl.BlockSpec((WINDOW, kv_w), lambda n, b: (prev(n, b), v_col)),
            pl.BlockSpec((rows, kv_w), lambda n, b: (cur(n, b), v_col)),
            _resident((2, N_HEADS, WINDOW, 2 * WINDOW)),
        ],
        out_specs=pl.BlockSpec((rows, D_MODEL), lambda n, b: (cur(n, b), 0)),
        out_shape=jax.ShapeDtypeStruct((m, D_MODEL), BF16),
        compiler_params=pltpu.CompilerParams(
            dimension_semantics=("arbitrary", "arbitrary"), vmem_limit_bytes=VMEM_LIMIT_BYTES),
        name="swa_attn",
    )(qkv, qkv, qkv, qkv, qkv, table)


def _fox_attn_kernel(q_ref, k_ref, v_ref, c_ref, ct_ref, o_ref):
    pg = pl.program_id(1)
    qi = pl.program_id(2)
    tq = TQ_FOX
    lane = lax.broadcasted_iota(jnp.int32, (1, LANES), 1)
    lo = lane < HEAD_DIM
    zero = jnp.zeros((), BF16)
    causal = (lax.broadcasted_iota(jnp.int32, (tq, tq), 0)
              >= lax.broadcasted_iota(jnp.int32, (tq, tq), 1))

    def attend(n_keys):
        c_tile = c_ref[...]
        n_heads = 2 * FOX_PAIRS_PER_STEP

        def scores(i):
            g, e = divmod(i, 2)
            q = q_ref[:, g * LANES:(g + 1) * LANES]
            qh = jnp.where(lo, q, zero) if e == 0 else jnp.where(lo, zero, q)
            k = k_ref[0:n_keys, g * LANES:(g + 1) * LANES]
            return lax.dot_general(qh, k, (((1,), (1,)), ((), ())), preferred_element_type=F32)

        s_next = scores(0)
        outs = []
        for i in range(n_heads):
            s = s_next
            if i + 1 < n_heads:
                s_next = scores(i + 1)
            g = i // 2
            head = 2 * FOX_PAIRS_PER_STEP * pg + i
            u = s - ct_ref[pl.ds(head, 1), 0:n_keys]
            diag = jnp.where(causal, u[:, n_keys - tq:], NEG)
            u = diag if n_keys == tq else jnp.concatenate([u[:, :n_keys - tq], diag], axis=1)
            cq = jnp.sum(jnp.where(lane == head, c_tile, 0.0), axis=1, keepdims=True)
            z_max = jnp.max(u, axis=1, keepdims=True) + cq
            p = jnp.exp2(u + (cq - z_max))
            l = jnp.sum(p, axis=1, keepdims=True)
            v = v_ref[0:n_keys, g * LANES:(g + 1) * LANES]
            o = jnp.dot(p.astype(BF16), v, preferred_element_type=F32)
            outs.append(o * (1.0 / l))
            if i % 2 == 1:
                o_ref[:, g * LANES:(g + 1) * LANES] = jnp.where(
                    lo, outs[i - 1], outs[i]).astype(o_ref.dtype)

    for i in range(FOX_Q_TILES):
        pl.when(qi == i)(functools.partial(attend, (i + 1) * tq))


def _fox_attention(qkv, c, ct, batch, seq):
    m = qkv.shape[0]
    nq = seq // TQ_FOX
    assert nq == FOX_Q_TILES
    width = FOX_PAIRS_PER_STEP * LANES
    k_col = D_MODEL // width
    v_col = 2 * k_col
    return pl.pallas_call(
        _fox_attn_kernel,
        grid=(batch, N_PAIRS // FOX_PAIRS_PER_STEP, nq),
        in_specs=[
            pl.BlockSpec((TQ_FOX, width), lambda b, p, i: (b * nq + i, p)),
            pl.BlockSpec((seq, width), lambda b, p, i: (b, k_col + p)),
            pl.BlockSpec((seq, width), lambda b, p, i: (b, v_col + p)),
            pl.BlockSpec((TQ_FOX, LANES), lambda b, p, i: (b * nq + i, 0)),
            pl.BlockSpec((N_HEADS, seq), lambda b, p, i: (0, b)),
        ],
        out_specs=pl.BlockSpec((TQ_FOX, width), lambda b, p, i: (b * nq + i, p)),
        out_shape=jax.ShapeDtypeStruct((m, D_MODEL), BF16),
        compiler_params=pltpu.CompilerParams(
            dimension_semantics=("arbitrary", "arbitrary", "arbitrary"),
            vmem_limit_bytes=VMEM_LIMIT_BYTES),
        name="fox_attn",
    )(qkv, qkv, qkv, c, ct)


def _post_kernel(has_bias, has_final, *refs):
    x_ref, a_ref, wo_ref = refs[:3]
    rest = list(refs[3:])
    bo_ref = rest.pop(0) if has_bias else None
    g_ref, wup_ref, wdown_ref = rest[:3]
    rest = rest[3:]
    gf_ref = rest.pop(0) if has_final else None
    o_ref = rest[0]

    y = x_ref[...] + jnp.dot(a_ref[...], wo_ref[...], preferred_element_type=F32)
    if has_bias:
        y = y + bo_ref[...]
    h = _rmsnorm(y, g_ref[...]).astype(BF16)
    acc = y
    for c in range(D_FF // FF_CHUNK):
        sl = slice(c * FF_CHUNK, (c + 1) * FF_CHUNK)
        u = jnp.dot(h, wup_ref[:, sl], preferred_element_type=F32)
        u = jnp.square(jnp.maximum(u, 0.0)).astype(BF16)
        acc = acc + jnp.dot(u, wdown_ref[sl, :], preferred_element_type=F32)
    if has_final:
        acc = _rmsnorm(acc, gf_ref[...])
    o_ref[...] = acc


def _post(x, a, wo, bo, g, wup, wdown, gf):
    m, d = x.shape
    row_spec = pl.BlockSpec((TM_POST, d), lambda i: (i, 0))
    args = [x, a, wo]
    specs = [row_spec, row_spec, _resident((d, d))]
    if bo is not None:
        args.append(bo.reshape(1, d))
        specs.append(_resident((1, d)))
    args += [g.reshape(1, d), wup, wdown]
    specs += [_resident((1, d)), _resident((d, D_FF)), _resident((D_FF, d))]
    if gf is not None:
        args.append(gf.reshape(1, d))
        specs.append(_resident((1, d)))
    return pl.pallas_call(
        functools.partial(_post_kernel, bo is not None, gf is not None),
        grid=(m // TM_POST,),
        in_specs=specs,
        out_specs=row_spec,
        out_shape=jax.ShapeDtypeStruct((m, d), F32),
        compiler_params=pltpu.CompilerParams(
            dimension_semantics=("arbitrary",), vmem_limit_bytes=VMEM_LIMIT_BYTES),
        name="post_final" if gf is not None else ("post_bias" if bo is not None else "post"),
    )(*args)


def _dup_kv_columns(w):
    lead = w.shape[:-1]
    w = w.reshape(lead + (N_KV_HEADS_A, 1, HEAD_DIM))
    w = jnp.broadcast_to(w, lead + (N_KV_HEADS_A, 2, HEAD_DIM))
    return w.reshape(lead + (N_KV_HEADS_A * LANES,))


def _swa_weights(w_qkv, b_qkv):
    nq = N_HEADS * HEAD_DIM
    nkv = N_KV_HEADS_A * HEAD_DIM

    def relayout(t):
        return jnp.concatenate([t[..., :nq], _dup_kv_columns(t[..., nq:nq + nkv]),
                                _dup_kv_columns(t[..., nq + nkv:])], axis=-1)

    return relayout(w_qkv).astype(BF16), relayout(b_qkv)


def kernel(x, rel_bias, norm_mix, norm_mlp, w_qkv_a, b_qkv_a, sinks_a, w_o_a, b_o_a,
           w_qkvf_b, b_f_b, w_o_b, w_up, w_down, norm_final):
    batch, seq, d = x.shape
    depth = norm_mix.shape[0]
    assert d == D_MODEL and seq % TM_PROJ == 0 and (batch * seq) % TM_POST == 0
    xf = x.reshape(batch * seq, d)
    hd = N_HEADS * HEAD_DIM
    for i in range(depth):
        j = i // 2
        gf = norm_final if i == depth - 1 else None
        if i % 2 == 0:
            w, b = _swa_weights(w_qkv_a[j], b_qkv_a[j])
            qkv = _swa_proj(xf, norm_mix[i], w, b)
            a = _swa_attention(qkv, _swa_table(rel_bias, sinks_a[j]), batch, seq)
            wo, bo = w_o_a[j].astype(BF16), b_o_a[j]
        else:
            w = w_qkvf_b[j, :, :3 * hd].astype(BF16)
            wf = jnp.pad(w_qkvf_b[j, :, 3 * hd:], ((0, 0), (0, LANES - N_HEADS))).astype(BF16)
            bf = jnp.pad(b_f_b[j], (0, LANES - N_HEADS)).reshape(1, LANES)
            qkv, c, ct = _fox_proj(xf, norm_mix[i], w, wf, bf, seq)
            a = _fox_attention(qkv, c, ct, batch, seq)
            wo, bo = w_o_b[j].astype(BF16), None
        xf = _post(xf, a, wo, bo, norm_mlp[i], w_up[i].astype(BF16), w_down[i].astype(BF16), gf)
    return xf.reshape(batch, seq, d)
```

```python
import functools
import math

import numpy as np
import jax
import jax.numpy as jnp
from jax import lax
from jax.experimental import pallas as pl
from jax.experimental.pallas import tpu as pltpu

D_MODEL = 1024
HEAD_DIM = 64
N_HEADS = 16
N_KV_HEADS_A = 4
GROUP_A = N_HEADS // N_KV_HEADS_A
WINDOW = 128
D_FF = 4 * D_MODEL
N_BUCKETS = 32
MAX_DISTANCE = 128
EPS = 1e-6
NEG = -1e30
SCALE = 1.0 / math.sqrt(HEAD_DIM)
LOG2E = math.log2(math.e)

LANES = 128
N_PAIRS = N_HEADS // 2
VMEM_LIMIT_BYTES = 56 * 1024 * 1024

TM_PROJ = 512
TM_POST = 512
FF_CHUNK = 1024
TQ_FOX = 256
FOX_Q_TILES = 2048 // TQ_FOX
FOX_PAIRS_PER_STEP = 4
SWA_BLOCKS_PER_STEP = 2
ONES_ROWS = 16

BF16 = jnp.bfloat16
F32 = jnp.float32


def _resident(shape):
    zeros = (0,) * len(shape)
    return pl.BlockSpec(shape, lambda *_: zeros, pipeline_mode=pl.Buffered(1))


def _rmsnorm(x, g):
    return x * lax.rsqrt(jnp.mean(x * x, axis=-1, keepdims=True) + EPS) * g


def _proj_chunks(h, w_ref, b_ref, o_ref, n_out, chunk, scaled_cols=0, col_scale=1.0):
    for c in range(n_out // chunk):
        sl = slice(c * chunk, (c + 1) * chunk)
        acc = jnp.dot(h, w_ref[:, sl], preferred_element_type=F32)
        if b_ref is not None:
            acc = acc + b_ref[:, sl]
        if (c + 1) * chunk <= scaled_cols:
            acc = acc * col_scale
        o_ref[:, sl] = acc.astype(o_ref.dtype)


def _swa_proj_kernel(x_ref, g_ref, w_ref, b_ref, o_ref):
    h = _rmsnorm(x_ref[...], g_ref[...]).astype(BF16)
    _proj_chunks(h, w_ref, b_ref, o_ref, o_ref.shape[1], 512,
                 scaled_cols=N_HEADS * HEAD_DIM, col_scale=SCALE * LOG2E)


def _swa_proj(x, g, w, b):
    m, d = x.shape
    n = w.shape[1]
    return pl.pallas_call(
        _swa_proj_kernel,
        grid=(m // TM_PROJ,),
        in_specs=[
            pl.BlockSpec((TM_PROJ, d), lambda i: (i, 0)),
            _resident((1, d)),
            _resident((d, n)),
            _resident((1, n)),
        ],
        out_specs=pl.BlockSpec((TM_PROJ, n), lambda i: (i, 0)),
        out_shape=jax.ShapeDtypeStruct((m, n), BF16),
        compiler_params=pltpu.CompilerParams(
            dimension_semantics=("arbitrary",), vmem_limit_bytes=VMEM_LIMIT_BYTES),
        name="swa_proj",
    )(x, g.reshape(1, d), w, b.reshape(1, n))


def _c_split_selectors():
    sel = np.zeros((3, LANES, LANES), np.float32)
    for j in range(3):
        for h in range(N_HEADS):
            sel[j, h, 3 * h + j] = 1.0
    return sel


def _fox_proj_kernel(tiles_per_seq, x_ref, g_ref, w_ref, wvt_ref, wf_ref, bf_ref, sel_ref,
                     o_ref, vt_ref, kc_ref, ct_ref, carry_ref):
    i = pl.program_id(0)

    @pl.when(i % tiles_per_seq == 0)
    def _():
        carry_ref[...] = jnp.zeros_like(carry_ref)

    h = _rmsnorm(x_ref[...], g_ref[...]).astype(BF16)
    _proj_chunks(h, w_ref, None, o_ref, o_ref.shape[1], 512,
                 scaled_cols=N_HEADS * HEAD_DIM, col_scale=SCALE * LOG2E)
    for r in range(vt_ref.shape[0] // 256):
        rows = slice(r * 256, (r + 1) * 256)
        vt_ref[rows, :] = lax.dot_general(
            wvt_ref[rows, :], h, (((1,), (1,)), ((), ())),
            preferred_element_type=F32).astype(vt_ref.dtype)

    fz = jnp.dot(h, wf_ref[...], preferred_element_type=F32) + bf_ref[...]
    cs = (jnp.minimum(fz, 0.0) - jnp.log1p(jnp.exp(-jnp.abs(fz)))) * LOG2E
    tm = cs.shape[0]
    row = lax.broadcasted_iota(jnp.int32, cs.shape, 0)
    shift = 1
    while shift < tm:
        cs = cs + jnp.where(row >= shift, pltpu.roll(cs, shift, axis=0), 0.0)
        shift *= 2
    cs = cs + carry_ref[...]
    carry_ref[...] = cs[tm - 1:tm, :]
    ct_ref[...] = cs.T[:N_HEADS, :]
    hi = cs.astype(BF16)
    mid = (cs - hi.astype(F32)).astype(BF16)
    lo = (cs - hi.astype(F32) - mid.astype(F32)).astype(BF16)
    kc = jnp.dot(hi, sel_ref[0], preferred_element_type=F32)
    kc = kc + jnp.dot(mid, sel_ref[1], preferred_element_type=F32)
    kc = kc + jnp.dot(lo, sel_ref[2], preferred_element_type=F32)
    kc_ref[...] = kc.astype(kc_ref.dtype)


def _fox_proj(x, g, w, wvt, wf, bf, seq):
    m, d = x.shape
    n = w.shape[1]
    nv = wvt.shape[0]
    sel = jnp.asarray(_c_split_selectors(), BF16)
    return pl.pallas_call(
        functools.partial(_fox_proj_kernel, seq // TM_PROJ),
        grid=(m // TM_PROJ,),
        in_specs=[
            pl.BlockSpec((TM_PROJ, d), lambda i: (i, 0)),
            _resident((1, d)),
            _resident((d, n)),
            _resident((nv, d)),
            _resident((d, LANES)),
            _resident((1, LANES)),
            _resident((3, LANES, LANES)),
        ],
        out_specs=[
            pl.BlockSpec((TM_PROJ, n), lambda i: (i, 0)),
            pl.BlockSpec((nv, TM_PROJ), lambda i: (0, i)),
            pl.BlockSpec((TM_PROJ, LANES), lambda i: (i, 0)),
            pl.BlockSpec((N_HEADS, TM_PROJ), lambda i: (0, i)),
        ],
        out_shape=[
            jax.ShapeDtypeStruct((m, n), BF16),
            jax.ShapeDtypeStruct((nv, m), BF16),
            jax.ShapeDtypeStruct((m, LANES), BF16),
            jax.ShapeDtypeStruct((N_HEADS, m), F32),
        ],
        scratch_shapes=[pltpu.VMEM((1, LANES), F32)],
        compiler_params=pltpu.CompilerParams(
            dimension_semantics=("arbitrary",), vmem_limit_bytes=VMEM_LIMIT_BYTES),
        name="fox_proj",
    )(x, g.reshape(1, d), w, wvt, wf, bf, sel)


def _t5_bucket_table():
    qi = np.arange(WINDOW)[:, None]
    kj = np.arange(2 * WINDOW)[None, :]
    dist = qi + WINDOW - kj
    max_exact = N_BUCKETS // 2
    d = np.maximum(dist, 0)
    dl = np.maximum(d, 1).astype(np.float32)
    large = max_exact + (np.log(dl / max_exact) / math.log(MAX_DISTANCE / max_exact)
                         * (N_BUCKETS - max_exact)).astype(np.int32)
    large = np.minimum(large, N_BUCKETS - 1)
    bucket = np.where(d < max_exact, d, large)
    in_window = (dist >= 0) & (dist < WINDOW)
    return np.where(in_window, bucket, -1).astype(np.int32)


def _swa_table_kernel(rel_ref, sink_ref, bucket_ref, o_ref):
    h = pl.program_id(0)
    bucket = bucket_ref[...]
    bias = jnp.zeros(bucket.shape, F32)
    for b in range(N_BUCKETS):
        bias = jnp.where(bucket == b, rel_ref[b, h], bias)
    col = lax.broadcasted_iota(jnp.int32, bucket.shape, 1)
    banded = jnp.where(bucket >= 0, bias * LOG2E, NEG)
    first = jnp.where(col >= WINDOW, banded, NEG)
    sink = sink_ref[0, h] * LOG2E
    o_ref[0, 0] = jnp.where(col == 0, sink, first)
    o_ref[1, 0] = jnp.where(col == 0, sink, banded)


def _swa_table(rel_bias, sinks):
    bucket = _t5_bucket_table()
    assert bucket[:, 0].max() < 0
    return pl.pallas_call(
        _swa_table_kernel,
        grid=(N_HEADS,),
        in_specs=[
            pl.BlockSpec(memory_space=pltpu.SMEM),
            pl.BlockSpec(memory_space=pltpu.SMEM),
            pl.BlockSpec((WINDOW, 2 * WINDOW), lambda h: (0, 0)),
        ],
        out_specs=pl.BlockSpec((2, 1, WINDOW, 2 * WINDOW), lambda h: (0, h, 0, 0)),
        out_shape=jax.ShapeDtypeStruct((2, N_HEADS, WINDOW, 2 * WINDOW), F32),
        compiler_params=pltpu.CompilerParams(dimension_semantics=("arbitrary",)),
        name="swa_table",
    )(rel_bias, sinks.reshape(1, N_HEADS), jnp.asarray(bucket))


def _swa_attn_kernel(q_ref, kp_ref, kc_ref, vp_ref, vc_ref, tab_ref, o_ref):
    n = pl.program_id(0)
    k_all = jnp.concatenate([kp_ref[...], kc_ref[...]], axis=0)
    v_all = jnp.concatenate([vp_ref[...], vc_ref[...]], axis=0)
    lo = lax.broadcasted_iota(jnp.int32, (1, LANES), 1) < HEAD_DIM
    zero = jnp.zeros((), BF16)
    key0 = lax.broadcasted_iota(jnp.int32, (2 * WINDOW, LANES), 0) == 0
    units = [(j, h) for j in range(SWA_BLOCKS_PER_STEP) for h in range(N_KV_HEADS_A)]

    def scores(unit):
        j, h = unit
        rows = slice(j * WINDOW, (j + 1) * WINDOW)
        q2 = jnp.concatenate([q_ref[rows, (2 * h) * LANES:(2 * h + 1) * LANES],
                              q_ref[rows, (2 * h + 1) * LANES:(2 * h + 2) * LANES]], axis=0)
        ql = jnp.concatenate([jnp.where(lo, q2, zero), jnp.where(lo, zero, q2)], axis=0)
        kd = k_all[j * WINDOW:(j + 2) * WINDOW, h * LANES:(h + 1) * LANES]
        kd = jnp.where(key0, zero, kd)
        return lax.dot_general(ql, kd, (((1,), (1,)), ((), ())), preferred_element_type=F32)

    s_next = scores(units[0])
    for idx, (j, h) in enumerate(units):
        s = s_next
        if idx + 1 < len(units):
            s_next = scores(units[idx + 1])
        heads = (4 * h, 4 * h + 2, 4 * h + 1, 4 * h + 3)
        variant = jnp.minimum(n, 1) if j == 0 else 1
        z = s + jnp.concatenate([tab_ref[variant, hd] for hd in heads], axis=0)
        p = jnp.exp2(z - jnp.max(z, axis=1, keepdims=True))
        den = jnp.sum(p, axis=1, keepdims=True)
        vd = v_all[j * WINDOW:(j + 2) * WINDOW, h * LANES:(h + 1) * LANES]
        vd = jnp.where(key0, zero, vd)
        o = jnp.dot(p.astype(BF16), vd, preferred_element_type=F32) * (1.0 / den)
        rows = slice(j * WINDOW, (j + 1) * WINDOW)
        o_ref[rows, (2 * h) * LANES:(2 * h + 1) * LANES] = jnp.where(
            lo, o[0:WINDOW], o[2 * WINDOW:3 * WINDOW]).astype(o_ref.dtype)
        o_ref[rows, (2 * h + 1) * LANES:(2 * h + 2) * LANES] = jnp.where(
            lo, o[WINDOW:2 * WINDOW], o[3 * WINDOW:4 * WINDOW]).astype(o_ref.dtype)


def _swa_attention(qkv, table, batch, seq):
    m = qkv.shape[0]
    rows = SWA_BLOCKS_PER_STEP * WINDOW
    steps = seq // rows
    kv_w = N_KV_HEADS_A * LANES
    k_col = D_MODEL // kv_w
    v_col = k_col + 1

    def cur(n, b):
        return b * steps + n

    def prev(n, b):
        return jnp.maximum((b * steps + n) * SWA_BLOCKS_PER_STEP - 1, 0)

    return pl.pallas_call(
        _swa_attn_kernel,
        grid=(steps, batch),
        in_specs=[
            pl.BlockSpec((rows, D_MODEL), lambda n, b: (cur(n, b), 0)),
            pl.BlockSpec((WINDOW, kv_w), lambda n, b: (prev(n, b), k_col)),
            pl.BlockSpec((rows, kv_w), lambda n, b: (cur(n, b), k_col)),
            pl.BlockSpec((WINDOW, kv_w), lambda n, b: (prev(n, b), v_col)),
            pl.BlockSpec((rows, kv_w), lambda n, b: (cur(n, b), v_col)),
            _resident((2, N_HEADS, WINDOW, 2 * WINDOW)),
        ],
        out_specs=pl.BlockSpec((rows, D_MODEL), lambda n, b: (cur(n, b), 0)),
        out_shape=jax.ShapeDtypeStruct((m, D_MODEL), BF16),
        compiler_params=pltpu.CompilerParams(
            dimension_semantics=("arbitrary", "arbitrary"), vmem_limit_bytes=VMEM_LIMIT_BYTES),
        name="swa_attn",
    )(qkv, qkv, qkv, qkv, qkv, table)


def _fox_attn_kernel(q_ref, k_ref, kc_ref, vt_ref, ct_ref, o_ref):
    pg = pl.program_id(1)
    qi = pl.program_id(2)
    tq = TQ_FOX
    lane = lax.broadcasted_iota(jnp.int32, (tq, LANES), 1)
    lo = lane < HEAD_DIM
    zero = jnp.zeros((), BF16)
    causal_t = (lax.broadcasted_iota(jnp.int32, (tq, 2 * tq), 0)
                <= lax.broadcasted_iota(jnp.int32, (tq, 2 * tq), 1) % tq)
    even_rows = lax.broadcasted_iota(jnp.int32, (LANES, 1), 0) < HEAD_DIM

    def attend(tile):
        n_keys = (tile + 1) * tq

        def scores(g):
            q = q_ref[:, g * LANES:(g + 1) * LANES]
            qa = []
            for e in range(2):
                head = 2 * (FOX_PAIRS_PER_STEP * pg + g) + e
                qh = jnp.where(lo, q, zero) if e == 0 else jnp.where(lo, zero, q)
                minus_one = jnp.where((lane >= 3 * head) & (lane < 3 * head + 3),
                                      -1.0, 0.0).astype(BF16)
                qa.append(jnp.concatenate([qh, minus_one], axis=1))
            ka = jnp.concatenate([k_ref[0:n_keys, g * LANES:(g + 1) * LANES],
                                  kc_ref[0:n_keys, :]], axis=1)
            return lax.dot_general(ka, jnp.concatenate(qa, axis=0), (((1,), (1,)), ((), ())),
                                   preferred_element_type=F32)

        s_next = scores(0)
        for g in range(FOX_PAIRS_PER_STEP):
            u = s_next
            if g + 1 < FOX_PAIRS_PER_STEP:
                s_next = scores(g + 1)
            head = 2 * (FOX_PAIRS_PER_STEP * pg + g)
            chunks = [u[c * tq:(c + 1) * tq, :] for c in range(tile)]
            chunks.append(jnp.where(causal_t, u[tile * tq:, :], NEG))
            ct = jnp.concatenate(
                [ct_ref[pl.ds(head + e, 1), tile * tq:(tile + 1) * tq] for e in range(2)],
                axis=1)
            col_max = jnp.max(chunks[0], axis=0, keepdims=True)
            for c in range(1, tile + 1):
                col_max = jnp.maximum(col_max, jnp.max(chunks[c], axis=0, keepdims=True))
            z_max = col_max + ct
            shift = ct - z_max
            p = jnp.concatenate([jnp.exp2(uc + shift).astype(BF16) for uc in chunks], axis=0)
            vt = jnp.concatenate([vt_ref[g * LANES:(g + 1) * LANES, 0:n_keys],
                                  jnp.ones((ONES_ROWS, n_keys), BF16)], axis=0)
            o = jnp.dot(vt, p, preferred_element_type=F32)
            o = o[:LANES] * (1.0 / o[LANES:LANES + 1])
            pair_t = jnp.where(even_rows, o[:, :tq], o[:, tq:])
            o_ref[:, g * LANES:(g + 1) * LANES] = pair_t.T.astype(o_ref.dtype)

    for i in range(FOX_Q_TILES):
        pl.when(qi == i)(functools.partial(attend, i))


def _fox_attention(qk, vt, kc, ct, batch, seq):
    m = qk.shape[0]
    nq = seq // TQ_FOX
    assert nq == FOX_Q_TILES
    width = FOX_PAIRS_PER_STEP * LANES
    k_col = D_MODEL // width
    return pl.pallas_call(
        _fox_attn_kernel,
        grid=(batch, N_PAIRS // FOX_PAIRS_PER_STEP, nq),
        in_specs=[
            pl.BlockSpec((TQ_FOX, width), lambda b, p, i: (b * nq + i, p)),
            pl.BlockSpec((seq, width), lambda b, p, i: (b, k_col + p)),
            pl.BlockSpec((seq, LANES), lambda b, p, i: (b, 0)),
            pl.BlockSpec((width, seq), lambda b, p, i: (p, b)),
            pl.BlockSpec((N_HEADS, seq), lambda b, p, i: (0, b)),
        ],
        out_specs=pl.BlockSpec((TQ_FOX, width), lambda b, p, i: (b * nq + i, p)),
        out_shape=jax.ShapeDtypeStruct((m, D_MODEL), BF16),
        compiler_params=pltpu.CompilerParams(
            dimension_semantics=("arbitrary", "arbitrary", "arbitrary"),
            vmem_limit_bytes=VMEM_LIMIT_BYTES,
        ),
        name="fox_attn",
    )(qk, qk, kc, vt, ct)


def _post_kernel(has_bias, has_final, *refs):
    x_ref, a_ref, wo_ref = refs[:3]
    rest = list(refs[3:])
    bo_ref = rest.pop(0) if has_bias else None
    g_ref, wup_ref, wdown_ref = rest[:3]
    rest = rest[3:]
    gf_ref = rest.pop(0) if has_final else None
    o_ref = rest[0]

    y = x_ref[...] + jnp.dot(a_ref[...], wo_ref[...], preferred_element_type=F32)
    if has_bias:
        y = y + bo_ref[...]
    h = _rmsnorm(y, g_ref[...]).astype(BF16)
    acc = y
    for c in range(D_FF // FF_CHUNK):
        sl = slice(c * FF_CHUNK, (c + 1) * FF_CHUNK)
        u = jnp.dot(h, wup_ref[:, sl], preferred_element_type=F32)
        u = jnp.square(jnp.maximum(u, 0.0)).astype(BF16)
        acc = acc + jnp.dot(u, wdown_ref[sl, :], preferred_element_type=F32)
    if has_final:
        acc = _rmsnorm(acc, gf_ref[...])
    o_ref[...] = acc


def _post(x, a, wo, bo, g, wup, wdown, gf):
    m, d = x.shape
    row_spec = pl.BlockSpec((TM_POST, d), lambda i: (i, 0))
    args = [x, a, wo]
    specs = [row_spec, row_spec, _resident((d, d))]
    if bo is not None:
        args.append(bo.reshape(1, d))
        specs.append(_resident((1, d)))
    args += [g.reshape(1, d), wup, wdown]
    specs += [_resident((1, d)), _resident((d, D_FF)), _resident((D_FF, d))]
    if gf is not None:
        args.append(gf.reshape(1, d))
        specs.append(_resident((1, d)))
    return pl.pallas_call(
        functools.partial(_post_kernel, bo is not None, gf is not None),
        grid=(m // TM_POST,),
        in_specs=specs,
        out_specs=row_spec,
        out_shape=jax.ShapeDtypeStruct((m, d), F32),
        compiler_params=pltpu.CompilerParams(
            dimension_semantics=("arbitrary",), vmem_limit_bytes=VMEM_LIMIT_BYTES),
        name="post_final" if gf is not None else ("post_bias" if bo is not None else "post"),
    )(*args)


def _dup_kv_columns(w):
    lead = w.shape[:-1]
    w = w.reshape(lead + (N_KV_HEADS_A, 1, HEAD_DIM))
    w = jnp.broadcast_to(w, lead + (N_KV_HEADS_A, 2, HEAD_DIM))
    return w.reshape(lead + (N_KV_HEADS_A * LANES,))


def _swa_weights(w_qkv, b_qkv):
    nq = N_HEADS * HEAD_DIM
    nkv = N_KV_HEADS_A * HEAD_DIM

    def relayout(t):
        return jnp.concatenate([t[..., :nq], _dup_kv_columns(t[..., nq:nq + nkv]),
                                _dup_kv_columns(t[..., nq + nkv:])], axis=-1)

    return relayout(w_qkv).astype(BF16), relayout(b_qkv)


def kernel(x, rel_bias, norm_mix, norm_mlp, w_qkv_a, b_qkv_a, sinks_a, w_o_a, b_o_a,
           w_qkvf_b, b_f_b, w_o_b, w_up, w_down, norm_final):
    batch, seq, d = x.shape
    depth = norm_mix.shape[0]
    assert d == D_MODEL and seq % TM_PROJ == 0 and (batch * seq) % TM_POST == 0
    xf = x.reshape(batch * seq, d)
    hd = N_HEADS * HEAD_DIM
    for i in range(depth):
        j = i // 2
        gf = norm_final if i == depth - 1 else None
        if i % 2 == 0:
            w, b = _swa_weights(w_qkv_a[j], b_qkv_a[j])
            qkv = _swa_proj(xf, norm_mix[i], w, b)
            a = _swa_attention(qkv, _swa_table(rel_bias, sinks_a[j]), batch, seq)
            wo, bo = w_o_a[j].astype(BF16), b_o_a[j]
        else:
            w = w_qkvf_b[j, :, :2 * hd].astype(BF16)
            wvt = w_qkvf_b[j, :, 2 * hd:3 * hd].T.astype(BF16)
            wf = jnp.pad(w_qkvf_b[j, :, 3 * hd:], ((0, 0), (0, LANES - N_HEADS))).astype(BF16)
            bf = jnp.pad(b_f_b[j], (0, LANES - N_HEADS)).reshape(1, LANES)
            qk, vt, kc, ct = _fox_proj(xf, norm_mix[i], w, wvt, wf, bf, seq)
            a = _fox_attention(qk, vt, kc, ct, batch, seq)
            wo, bo = w_o_b[j].astype(BF16), None
        xf = _post(xf, a, wo, bo, norm_mlp[i], w_up[i].astype(BF16), w_down[i].astype(BF16), gf)
    return xf.reshape(batch, seq, d)
```
